```python
import math
import jax, jax.numpy as jnp
from jax import lax
import numpy as np

D_MODEL = 2048
BATCH = 4
SEQ = 2048
DEPTH = 2
DEC_BATCH = 8
DEC_SEQ = 1
PAST_LEN = 16384
PAGE_SIZE = 128

GROUP_W = D_MODEL // 4
N_MIXERS = 4
MIX_W = N_MIXERS * GROUP_W
FOX_HEAD_DIM = 64
FOX_HEADS = GROUP_W // FOX_HEAD_DIM
Q_BLOCK = 128
SSD_HEAD_DIM = 64
SSD_HEADS = GROUP_W // SSD_HEAD_DIM
SSD_STATE = 128
SSD_GROUPS = 2
SSD_CONV = 4
SSD_CHUNK = 128
SSD_CONV_DIM = GROUP_W + 2 * SSD_GROUPS * SSD_STATE
GMLP_CHUNK = 128
GMLP_GROUPS = 4
GMLP_GROUP_CH = GROUP_W // GMLP_GROUPS
CONF_KERNEL = 31
IN_SIZES = (GROUP_W, GROUP_W, GROUP_W, FOX_HEADS,
            GROUP_W, SSD_CONV_DIM, SSD_HEADS,
            GROUP_W, GROUP_W,
            GROUP_W, GROUP_W)
IN_COLS = sum(IN_SIZES)
IN_SPLITS = tuple(sum(IN_SIZES[:i + 1]) for i in range(len(IN_SIZES) - 1))
N_EXPERTS = 32
TOP_K = 4
D_FF = D_MODEL
SWIGLU_LIMIT = 7.0
SWIGLU_ALPHA = 1.702
MOE_BLOCK = 128
DEEPNORM_ALPHA = (2 * DEPTH) ** 0.25
DEEPNORM_BETA = (8 * DEPTH) ** -0.25
EPS = 1e-5

kernel_name = 'hymba_fox_ssd_gmlp_conformer_moe_step'

F32 = jnp.float32


def layer_norm(x, g, b):
    xf = x.astype(F32)
    mu = jnp.mean(xf, axis=-1, keepdims=True)
    xc = xf - mu
    var = jnp.mean(xc * xc, axis=-1, keepdims=True)
    return (xc * lax.rsqrt(var + EPS) * g.astype(F32) + b.astype(F32)).astype(x.dtype)


def group_rms_norm(y, g):
    yf = y.astype(F32).reshape(*y.shape[:-1], N_MIXERS, GROUP_W)
    yf = yf * lax.rsqrt(jnp.mean(yf * yf, axis=-1, keepdims=True) + EPS)
    return (yf.reshape(y.shape) * g.astype(F32)).astype(y.dtype)


def causal_dwconv(x_full, w, b):
    c = x_full.shape[-1]
    out = lax.conv_general_dilated(x_full, w[:, None, :].astype(x_full.dtype), window_strides=(1,),
                                   padding='VALID', dimension_numbers=('NWC', 'WIO', 'NWC'),
                                   feature_group_count=c)
    return out + b.astype(out.dtype)


def fox_attend(q, cq, qpos, k, v, ck, kpos):
    s = jnp.einsum('bqhd,bshd->bhqs', q, k, preferred_element_type=F32) * (FOX_HEAD_DIM ** -0.5)
    s = s + jnp.transpose(cq, (0, 2, 1))[..., None] - jnp.transpose(ck, (0, 2, 1))[:, :, None, :]
    s = jnp.where(kpos[None, :] <= qpos[:, None], s, -jnp.inf)
    p = jax.nn.softmax(s, axis=-1)
    return jnp.einsum('bhqs,bshd->bqhd', p.astype(v.dtype), v)


def fox_prompt(q, k, v, logf):
    bsz, seq = q.shape[:2]
    nb = seq // Q_BLOCK
    c = jnp.cumsum(logf, axis=1)
    pos = jnp.arange(seq)
    qb = jnp.moveaxis(q.reshape(bsz, nb, Q_BLOCK, FOX_HEADS, FOX_HEAD_DIM), 1, 0)
    cb = jnp.moveaxis(c.reshape(bsz, nb, Q_BLOCK, FOX_HEADS), 1, 0)
    pb = pos.reshape(nb, Q_BLOCK)
    out = lax.map(lambda blk: fox_attend(blk[0], blk[1], blk[2], k, v, c, pos), (qb, cb, pb))
    return jnp.moveaxis(out, 0, 1).reshape(bsz, seq, GROUP_W)


def fox_sample(q, k, v, logf, k_pages, v_pages, logf_pages, page_table):
    bsz, t_new = q.shape[:2]
    k_past = k_pages[page_table].reshape(bsz, -1, FOX_HEADS, FOX_HEAD_DIM).astype(k.dtype)
    v_past = v_pages[page_table].reshape(bsz, -1, FOX_HEADS, FOX_HEAD_DIM).astype(v.dtype)
    lf_past = logf_pages[page_table].reshape(bsz, -1, FOX_HEADS).astype(F32)
    past = k_past.shape[1]
    k_all = jnp.concatenate([k_past, k], axis=1)
    v_all = jnp.concatenate([v_past, v], axis=1)
    c = jnp.cumsum(jnp.concatenate([lf_past, logf], axis=1), axis=1)
    kpos = jnp.arange(past + t_new)
    qpos = past + jnp.arange(t_new)
    out = fox_attend(q, c[:, past:], qpos, k_all, v_all, c, kpos)
    return out.reshape(bsz, t_new, GROUP_W)


def ssd_chunked(x, dt, a, bm, cm):
    bsz, seq = x.shape[:2]
    nc = seq // SSD_CHUNK
    rep = SSD_HEADS // SSD_GROUPS
    xc = x.astype(F32).reshape(bsz, nc, SSD_CHUNK, SSD_GROUPS, rep, SSD_HEAD_DIM)
    dtc = dt.reshape(bsz, nc, SSD_CHUNK, SSD_GROUPS, rep)
    bc = bm.astype(F32).reshape(bsz, nc, SSD_CHUNK, SSD_GROUPS, SSD_STATE)
    cc = cm.astype(F32).reshape(bsz, nc, SSD_CHUNK, SSD_GROUPS, SSD_STATE)
    acs = jnp.cumsum(dtc * a.reshape(SSD_GROUPS, rep), axis=2)
    seg = acs[:, :, :, None] - acs[:, :, None, :]
    tri = jnp.tril(jnp.ones((SSD_CHUNK, SSD_CHUNK), dtype=bool))
    decay = jnp.exp(jnp.where(tri[:, :, None, None], seg, -jnp.inf))
    cb = jnp.einsum('bctgn,bcsgn->bctsg', cc, bc)
    m = cb[..., None] * decay * dtc[:, :, None]
    y_diag = jnp.einsum('bctsgr,bcsgrp->bctgrp', m, xc)
    xw = xc * (jnp.exp(acs[:, :, -1:] - acs) * dtc)[..., None]
    states = jnp.einsum('bcsgn,bcsgrp->bcgrpn', bc, xw)
    chunk_decay = jnp.exp(acs[:, :, -1])

    def step(h, inp):
        st, dec = inp
        return h * dec[..., None, None] + st, h

    h0 = jnp.zeros((bsz, SSD_GROUPS, rep, SSD_HEAD_DIM, SSD_STATE), F32)
    h_fin, h_prev = lax.scan(step, h0, (jnp.moveaxis(states, 1, 0), jnp.moveaxis(chunk_decay, 1, 0)))
    h_prev = jnp.moveaxis(h_prev, 0, 1)
    y_off = jnp.einsum('bctgn,bcgrpn->bctgrp', cc, h_prev) * jnp.exp(acs)[..., None]
    y = (y_diag + y_off).reshape(bsz, seq, SSD_HEADS, SSD_HEAD_DIM)
    return y, h_fin.reshape(bsz, SSD_HEADS, SSD_HEAD_DIM, SSD_STATE)


def ssd_recurrent(x, dt, a, bm, cm, h0):
    rep = SSD_HEADS // SSD_GROUPS
    bh = jnp.repeat(bm.astype(F32), rep, axis=2)
    ch = jnp.repeat(cm.astype(F32), rep, axis=2)

    def step(h, inp):
        xt, dtt, bt, ct = inp
        h = h * jnp.exp(dtt * a)[:, :, None, None] + (dtt[:, :, None] * xt)[..., None] * bt[:, :, None, :]
        return h, jnp.einsum('bhpn,bhn->bhp', h, ct)

    tm = lambda t: jnp.swapaxes(t, 0, 1)
    h_fin, ys = lax.scan(step, h0.astype(F32), (tm(x.astype(F32)), tm(dt), tm(bh), tm(ch)))
    return tm(ys), h_fin


def gmlp_mix(u, v, ln_g, ln_b, ws, bs):
    bsz, seq, _ = u.shape
    n = min(seq, GMLP_CHUNK)
    v = layer_norm(jax.nn.gelu(v), ln_g, ln_b)
    wm = ws[:, :n, :n] * jnp.tril(jnp.ones((n, n), ws.dtype))
    vc = v.reshape(bsz, seq // n, n, GMLP_GROUPS, GMLP_GROUP_CH)
    s = jnp.einsum('gts,bcsgd->bctgd', wm.astype(v.dtype), vc) + bs[:, :n].T.astype(v.dtype)[None, None, :, :, None]
    return jax.nn.gelu(u) * s.reshape(bsz, seq, GROUP_W), v


def conformer_conv(a, gate, prev, w, b, ln_g, ln_b):
    g = a * jax.nn.sigmoid(gate)
    full = jnp.concatenate([prev.astype(g.dtype), g], axis=1)
    h = jax.nn.silu(layer_norm(causal_dwconv(full, w, b), ln_g, ln_b))
    return h, full[:, -(CONF_KERNEL - 1):]


def token_mixer(x, P, past):
    bsz, seq, _ = x.shape
    (q, k, v, f, z, xbc, dt_raw, gu, gv, glu_a, glu_b) = jnp.split(x @ P['w_in'], IN_SPLITS, axis=-1)
    q = q.reshape(bsz, seq, FOX_HEADS, FOX_HEAD_DIM)
    k = k.reshape(bsz, seq, FOX_HEADS, FOX_HEAD_DIM)
    v = v.reshape(bsz, seq, FOX_HEADS, FOX_HEAD_DIM)
    logf = jax.nn.log_sigmoid(f.astype(F32) + P['b_forget'].astype(F32))
    if past is None:
        a_out = fox_prompt(q, k, v, logf)
        ssd_prev = jnp.zeros((bsz, SSD_CONV - 1, SSD_CONV_DIM), x.dtype)
        conf_prev = jnp.zeros((bsz, CONF_KERNEL - 1, GROUP_W), x.dtype)
    else:
        a_out = fox_sample(q, k, v, logf, past['cache_k'], past['cache_v'], past['cache_logf'], past['page_table'])
        ssd_prev = past['state_ssd_conv']
        conf_prev = past['state_conv']
    xbc_full = jnp.concatenate([ssd_prev.astype(xbc.dtype), xbc], axis=1)
    new_ssd_conv = xbc_full[:, -(SSD_CONV - 1):]
    xbc_c = jax.nn.silu(causal_dwconv(xbc_full, P['ssd_conv_w'], P['ssd_conv_b']))
    xs, bm, cm = jnp.split(xbc_c, [GROUP_W, GROUP_W + SSD_GROUPS * SSD_STATE], axis=-1)
    xs = xs.reshape(bsz, seq, SSD_HEADS, SSD_HEAD_DIM)
    bm = bm.reshape(bsz, seq, SSD_GROUPS, SSD_STATE)
    cm = cm.reshape(bsz, seq, SSD_GROUPS, SSD_STATE)
    dt = jax.nn.softplus(dt_raw.astype(F32) + P['ssd_dt_bias'].astype(F32))
    a = -jnp.exp(P['ssd_a_log'].astype(F32))
    if past is None:
        y_ssd, h_fin = ssd_chunked(xs, dt, a, bm, cm)
    else:
        y_ssd, h_fin = ssd_recurrent(xs, dt, a, bm, cm, past['state_ssm'])
    y_ssd = y_ssd + P['ssd_d'].astype(F32)[:, None] * xs.astype(F32)
    b_out = y_ssd.reshape(bsz, seq, GROUP_W).astype(x.dtype) * jax.nn.silu(z)
    c_out, v_rows = gmlp_mix(gu, gv, P['gmlp_ln_g'], P['gmlp_ln_b'], P['gmlp_ws'], P['gmlp_bs'])
    d_out, new_conf = conformer_conv(glu_a, glu_b, conf_prev, P['conf_conv_w'], P['conf_conv_b'],
                                     P['conf_ln_g'], P['conf_ln_b'])
    mixed = jnp.concatenate([a_out.astype(x.dtype), b_out, c_out, d_out], axis=-1)
    y = group_rms_norm(mixed, P['mix_norm_g']) @ P['w_out']
    state = (k, v, logf.astype(x.dtype), h_fin.astype(x.dtype), new_ssd_conv, new_conf, v_rows)
    return y, state


def moe_ffn(x, router_w, router_b, w_gu, b_gu, w_dn, b_dn):
    bsz, seq, d = x.shape
    xt = x.reshape(-1, d)
    n_tok = xt.shape[0]
    logits = (xt @ router_w).astype(F32) + router_b.astype(F32)
    top_v, top_i = lax.top_k(logits, TOP_K)
    gates = jax.nn.softmax(top_v, axis=-1)
    n_pairs = n_tok * TOP_K
    blk = min(MOE_BLOCK, n_pairs)
    n_blocks = -(-n_pairs // blk) + N_EXPERTS
    flat_e = top_i.reshape(-1)
    order = jnp.argsort(flat_e)
    sorted_e = flat_e[order]
    sorted_tok = order // TOP_K
    counts = jnp.bincount(flat_e, length=N_EXPERTS)
    padded = (counts + blk - 1) // blk * blk
    pad_end = jnp.cumsum(padded)
    pad_start = pad_end - padded
    start = jnp.cumsum(counts) - counts
    dest = pad_start[sorted_e] + jnp.arange(n_pairs) - start[sorted_e]
    row_tok = jnp.full((n_blocks * blk,), n_tok, jnp.int32).at[dest].set(sorted_tok.astype(jnp.int32))
    x_rows = jnp.concatenate([xt, jnp.zeros((1, d), xt.dtype)], axis=0)[row_tok].reshape(n_blocks, blk, d)
    block_e = jnp.minimum(jnp.sum(pad_end[None, :] <= (jnp.arange(n_blocks) * blk)[:, None], axis=1), N_EXPERTS - 1)

    def expert_block(args):
        xb, e = args
        h = xb @ w_gu[e] + b_gu[e]
        gate, up = jnp.split(h, 2, axis=-1)
        gate = jnp.minimum(gate, SWIGLU_LIMIT)
        up = jnp.clip(up, -SWIGLU_LIMIT, SWIGLU_LIMIT)
        act = (up + 1.0) * (gate * jax.nn.sigmoid(SWIGLU_ALPHA * gate))
        return act @ w_dn[e] + b_dn[e]

    y_rows = lax.map(expert_block, (x_rows, block_e)).reshape(-1, d)
    y_pairs = y_rows[dest] * gates.reshape(-1)[order][:, None].astype(y_rows.dtype)
    out = jax.ops.segment_sum(y_pairs, sorted_tok, num_segments=n_tok)
    return out.reshape(bsz, seq, d)


def hybrid_layer(x, P, past):
    mix, state = token_mixer(x, P, past)
    h = layer_norm(DEEPNORM_ALPHA * x + mix, P['ln1_g'], P['ln1_b'])
    ffn = moe_ffn(h, P['router_w'], P['router_b'], P['exp_w_gu'], P['exp_b_gu'], P['exp_w_dn'], P['exp_b_dn'])
    return layer_norm(DEEPNORM_ALPHA * h + ffn, P['ln2_g'], P['ln2_b']), state


def setup_inputs(seed: int = 0) -> dict:
    key = jax.random.key(seed)
    ks = jax.random.split(key, 40)
    nrm = lambda i, shape, scale: jax.random.normal(ks[i], shape, F32) * scale
    n_pages = PAST_LEN // PAGE_SIZE
    n_pool = (5 * DEC_BATCH * n_pages + 3) // 4
    page_table = jax.random.permutation(ks[0], n_pool)[:DEC_BATCH * n_pages].reshape(DEC_BATCH, n_pages).astype(jnp.int32)
    dt0 = jnp.exp(jax.random.uniform(ks[1], (DEPTH, SSD_HEADS), F32, math.log(1e-3), math.log(1e-1)))
    return {
        'x_prompt': nrm(2, (BATCH, SEQ, D_MODEL), 1.0),
        'x_sample': nrm(3, (DEC_BATCH, DEC_SEQ, D_MODEL), 1.0),
        'cache_k': nrm(4, (DEPTH, n_pool, PAGE_SIZE, FOX_HEADS, FOX_HEAD_DIM), 1.0),
        'cache_v': nrm(5, (DEPTH, n_pool, PAGE_SIZE, FOX_HEADS, FOX_HEAD_DIM), 1.0),
        'cache_logf': jax.nn.log_sigmoid(3.0 + nrm(6, (DEPTH, n_pool, PAGE_SIZE, FOX_HEADS), 1.0)),
        'state_ssm': nrm(7, (DEPTH, DEC_BATCH, SSD_HEADS, SSD_HEAD_DIM, SSD_STATE), 0.1),
        'state_ssd_conv': nrm(8, (DEPTH, DEC_BATCH, SSD_CONV - 1, SSD_CONV_DIM), 1.0),
        'state_conv': nrm(9, (DEPTH, DEC_BATCH, CONF_KERNEL - 1, GROUP_W), 1.0),
        'page_table': page_table,
        'w_in': nrm(10, (DEPTH, D_MODEL, IN_COLS), D_MODEL ** -0.5),
        'b_forget': 3.0 + nrm(11, (DEPTH, FOX_HEADS), 0.5),
        'ssd_conv_w': nrm(12, (DEPTH, SSD_CONV, SSD_CONV_DIM), SSD_CONV ** -0.5),
        'ssd_conv_b': nrm(13, (DEPTH, SSD_CONV_DIM), 0.02),
        'ssd_dt_bias': dt0 + jnp.log(-jnp.expm1(-dt0)),
        'ssd_a_log': jnp.log(jax.random.uniform(ks[14], (DEPTH, SSD_HEADS), F32, 1.0, 16.0)),
        'ssd_d': 1.0 + nrm(15, (DEPTH, SSD_HEADS), 0.1),
        'gmlp_ln_g': 1.0 + nrm(16, (DEPTH, GROUP_W), 0.1),
        'gmlp_ln_b': nrm(17, (DEPTH, GROUP_W), 0.1),
        'gmlp_ws': nrm(18, (DEPTH, GMLP_GROUPS, GMLP_CHUNK, GMLP_CHUNK), GMLP_CHUNK ** -0.5),
        'gmlp_bs': 1.0 + nrm(19, (DEPTH, GMLP_GROUPS, GMLP_CHUNK), 0.1),
        'conf_conv_w': nrm(20, (DEPTH, CONF_KERNEL, GROUP_W), CONF_KERNEL ** -0.5),
        'conf_conv_b': nrm(21, (DEPTH, GROUP_W), 0.02),
        'conf_ln_g': 1.0 + nrm(22, (DEPTH, GROUP_W), 0.1),
        'conf_ln_b': nrm(23, (DEPTH, GROUP_W), 0.1),
        'mix_norm_g': 1.0 + nrm(24, (DEPTH, MIX_W), 0.1),
        'w_out': nrm(25, (DEPTH, MIX_W, D_MODEL), MIX_W ** -0.5 * DEEPNORM_BETA),
        'ln1_g': 1.0 + nrm(26, (DEPTH, D_MODEL), 0.1),
        'ln1_b': nrm(27, (DEPTH, D_MODEL), 0.02),
        'router_w': nrm(28, (DEPTH, D_MODEL, N_EXPERTS), D_MODEL ** -0.5),
        'router_b': nrm(29, (DEPTH, N_EXPERTS), 0.01),
        'exp_w_gu': nrm(30, (DEPTH, N_EXPERTS, D_MODEL, 2 * D_FF), D_MODEL ** -0.5),
        'exp_b_gu': nrm(31, (DEPTH, N_EXPERTS, 2 * D_FF), 0.02),
        'exp_w_dn': nrm(32, (DEPTH, N_EXPERTS, D_FF, D_MODEL), D_FF ** -0.5 * DEEPNORM_BETA),
        'exp_b_dn': nrm(33, (DEPTH, N_EXPERTS, D_MODEL), 0.02),
        'ln2_g': 1.0 + nrm(34, (DEPTH, D_MODEL), 0.1),
        'ln2_b': nrm(35, (DEPTH, D_MODEL), 0.02),
    }


def reference(x_prompt, x_sample, cache_k, cache_v, cache_logf, state_ssm, state_ssd_conv, state_conv, page_table,
              w_in, b_forget, ssd_conv_w, ssd_conv_b, ssd_dt_bias, ssd_a_log, ssd_d, gmlp_ln_g, gmlp_ln_b,
              gmlp_ws, gmlp_bs, conf_conv_w, conf_conv_b, conf_ln_g, conf_ln_b, mix_norm_g, w_out, ln1_g, ln1_b,
              router_w, router_b, exp_w_gu, exp_b_gu, exp_w_dn, exp_b_dn, ln2_g, ln2_b):
    yp, ys = x_prompt, x_sample
    st_p, st_s = [], []
    for l in range(DEPTH):
        P = dict(w_in=w_in[l], b_forget=b_forget[l], ssd_conv_w=ssd_conv_w[l], ssd_conv_b=ssd_conv_b[l],
                 ssd_dt_bias=ssd_dt_bias[l], ssd_a_log=ssd_a_log[l], ssd_d=ssd_d[l], gmlp_ln_g=gmlp_ln_g[l],
                 gmlp_ln_b=gmlp_ln_b[l], gmlp_ws=gmlp_ws[l], gmlp_bs=gmlp_bs[l], conf_conv_w=conf_conv_w[l],
                 conf_conv_b=conf_conv_b[l], conf_ln_g=conf_ln_g[l], conf_ln_b=conf_ln_b[l],
                 mix_norm_g=mix_norm_g[l], w_out=w_out[l], ln1_g=ln1_g[l], ln1_b=ln1_b[l],
                 router_w=router_w[l], router_b=router_b[l], exp_w_gu=exp_w_gu[l], exp_b_gu=exp_b_gu[l],
                 exp_w_dn=exp_w_dn[l], exp_b_dn=exp_b_dn[l], ln2_g=ln2_g[l], ln2_b=ln2_b[l])
        past = dict(cache_k=cache_k[l], cache_v=cache_v[l], cache_logf=cache_logf[l], state_ssm=state_ssm[l],
                    state_ssd_conv=state_ssd_conv[l], state_conv=state_conv[l], page_table=page_table)
        yp, sp = hybrid_layer(yp, P, None)
        ys, ss = hybrid_layer(ys, P, past)
        st_p.append(sp)
        st_s.append(ss)
    stack = lambda sts, i: jnp.stack([s[i] for s in sts], axis=0)
    k_prompt, v_prompt, logf_prompt = stack(st_p, 0), stack(st_p, 1), stack(st_p, 2)
    k_sample, v_sample, logf_sample = stack(st_s, 0), stack(st_s, 1), stack(st_s, 2)
    ssm_prompt, ssm_sample = stack(st_p, 3), stack(st_s, 3)
    ssd_conv_prompt, ssd_conv_sample = stack(st_p, 4), stack(st_s, 4)
    conf_conv_prompt, conf_conv_sample = stack(st_p, 5), stack(st_s, 5)
    gmlp_v_sample = stack(st_s, 6)
    return (yp, ys, k_prompt, v_prompt, logf_prompt, k_sample, v_sample, logf_sample, ssm_prompt, ssm_sample,
            ssd_conv_prompt, ssd_conv_sample, conf_conv_prompt, conf_conv_sample, gmlp_v_sample)
```

```python
import functools
import math

import jax
import jax.numpy as jnp
from jax import lax
from jax.experimental import pallas as pl
from jax.experimental.pallas import tpu as pltpu

F32 = jnp.float32
BF16 = jnp.bfloat16

D_MODEL = 2048
GROUP_W = 512
N_MIXERS = 4
FOX_HEADS = 8
FOX_HEAD_DIM = 64
SSD_HEADS = 8
SSD_HEAD_DIM = 64
SSD_STATE = 128
SSD_GROUPS = 2
SSD_CONV = 4
SSD_CHUNK = 128
SSD_CONV_DIM = GROUP_W + 2 * SSD_GROUPS * SSD_STATE
GMLP_CHUNK = 128
GMLP_GROUPS = 4
CONF_KERNEL = 31
IN_SIZES = (GROUP_W, GROUP_W, GROUP_W, FOX_HEADS, GROUP_W, SSD_CONV_DIM, SSD_HEADS,
            GROUP_W, GROUP_W, GROUP_W, GROUP_W)
IN_SPLITS = tuple(sum(IN_SIZES[:i + 1]) for i in range(len(IN_SIZES) - 1))
N_EXPERTS = 32
TOP_K = 4
D_FF = D_MODEL
SWIGLU_LIMIT = 7.0
SWIGLU_ALPHA = 1.702
DEPTH = 2
DEEPNORM_ALPHA = (2 * DEPTH) ** 0.25
EPS = 1e-5
PAGE_SIZE = 128

LANES = 128
SUBLANES = 8
VMEM_LIMIT = 56 * 1024 * 1024

COL_Q, COL_K, COL_V, COL_Z, COL_XBC, COL_GU, COL_GV, COL_GA, COL_GB, COL_MISC = (
    0, 512, 1024, 1536, 2048, 3072, 3584, 4096, 4608, 5120)
MISC_F = 0
MISC_DT = 8
IN_COLS_PAD = 5376
IN_TN = 1792
NEG = -1e30

MOE_R = 1152
MOE_SB = 384
MOE_TF = 256


def _cp(sem, vmem=VMEM_LIMIT):
    return pltpu.CompilerParams(dimension_semantics=sem, vmem_limit_bytes=vmem)


def _split3(x):
    hi = x.astype(BF16)
    r1 = x - hi.astype(F32)
    mid = r1.astype(BF16)
    lo = (r1 - mid.astype(F32)).astype(BF16)
    return hi, mid, lo


def _dot01_left(t01, x):
    hi, mid, lo = _split3(x)
    d = lambda b: jnp.dot(t01, b, preferred_element_type=F32)
    return d(hi) + d(mid) + d(lo)


def _dot01_right(x, t01):
    hi, mid, lo = _split3(x)
    d = lambda a: jnp.dot(a, t01, preferred_element_type=F32)
    return d(hi) + d(mid) + d(lo)


def _log_sigmoid(x):
    return jnp.minimum(x, 0.0) - jnp.log(1.0 + jnp.exp(-jnp.abs(x)))


def _softplus(x):
    return jnp.maximum(x, 0.0) + jnp.log(1.0 + jnp.exp(-jnp.abs(x)))


def _sigmoid(x):
    return 1.0 / (1.0 + jnp.exp(-x))


def _silu(x):
    return x * _sigmoid(x)


def _gelu(x):
    return 0.5 * x * (1.0 + jnp.tanh(math.sqrt(2.0 / math.pi) * (x + 0.044715 * (x * x * x))))


def _layer_norm(x, g, b):
    mu = jnp.mean(x, axis=-1, keepdims=True)
    xc = x - mu
    var = jnp.mean(xc * xc, axis=-1, keepdims=True)
    return xc * lax.rsqrt(var + EPS) * g + b


def _mm_kernel(x_ref, w_ref, o_ref):
    o_ref[...] = jnp.dot(x_ref[...].astype(BF16), w_ref[...], preferred_element_type=F32)


def _matmul(x, w, tm, tn):
    m, k = x.shape
    n = w.shape[1]
    return pl.pallas_call(
        _mm_kernel,
        grid=(n // tn, pl.cdiv(m, tm)),
        in_specs=[pl.BlockSpec((tm, k), lambda j, i: (i, 0)),
                  pl.BlockSpec((k, tn), lambda j, i: (0, j))],
        out_specs=pl.BlockSpec((tm, tn), lambda j, i: (i, j)),
        out_shape=jax.ShapeDtypeStruct((m, n), F32),
        compiler_params=_cp(("arbitrary", "arbitrary")),
        name="in_proj",
    )(x, w)


def _prep_w_in(w):
    q, k, v, f, z, xbc, dt, gu, gv, ga, gb = jnp.split(w, IN_SPLITS, axis=-1)
    d = w.shape[0]
    misc = jnp.concatenate([f, dt, jnp.zeros((d, LANES - 16), w.dtype)], axis=-1)
    pad = jnp.zeros((d, IN_COLS_PAD - (COL_MISC + LANES)), w.dtype)
    return jnp.concatenate([q, k, v, z, xbc, gu, gv, ga, gb, misc, pad], axis=-1).astype(BF16)


FOX_CH = 256


def _fox_prep_kernel(misc_ref, bf_ref, lf_ref, c_ref, ct_ref, carry):
    ch = misc_ref.shape[0]

    @pl.when(pl.program_id(1) == 0)
    def _():
        carry[...] = jnp.zeros_like(carry)

    lf = _log_sigmoid(misc_ref[...] + bf_ref[...])
    row = lax.broadcasted_iota(jnp.int32, (ch, ch), 0)
    col = lax.broadcasted_iota(jnp.int32, (ch, ch), 1)
    tri = jnp.where(row >= col, 1.0, 0.0).astype(BF16)
    cs = _dot01_left(tri, lf) + carry[...]
    carry[...] = cs[ch - 1:ch, :]
    lf_ref[...] = lf
    c_ref[...] = cs
    ct_ref[...] = cs.T[0:SUBLANES, :]


def _fox_prep(h, bf_row, nb, seq):
    ch = FOX_CH
    nch = seq // ch
    t = nb * seq
    return pl.pallas_call(
        _fox_prep_kernel,
        grid=(nb, nch),
        in_specs=[pl.BlockSpec((ch, LANES), lambda b, c: (b * nch + c, COL_MISC // LANES)),
                  pl.BlockSpec((1, LANES), lambda b, c: (0, 0))],
        out_specs=[pl.BlockSpec((ch, LANES), lambda b, c: (b * nch + c, 0)),
                   pl.BlockSpec((ch, LANES), lambda b, c: (b * nch + c, 0)),
                   pl.BlockSpec((None, None, SUBLANES, ch), lambda b, c: (b, c, 0, 0))],
        out_shape=[jax.ShapeDtypeStruct((t, LANES), F32),
                   jax.ShapeDtypeStruct((t, LANES), F32),
                   jax.ShapeDtypeStruct((nb, nch, SUBLANES, ch), F32)],
        scratch_shapes=[pltpu.VMEM((1, LANES), F32)],
        compiler_params=_cp(("arbitrary", "arbitrary")),
        name="fox_prep",
    )(h, bf_row)


def _fox_kernel(q_ref, k_ref, v_ref, c_ref, ct_ref, o_ref, kbf, vbf):
    tq = q_ref.shape[0]
    qi = pl.program_id(1)

    @pl.when(qi == 0)
    def _():
        kbf[...] = k_ref[...].astype(BF16)
        vbf[...] = v_ref[...].astype(BF16)

    q_all = (q_ref[...] * (FOX_HEAD_DIM ** -0.5)).astype(BF16)
    c_all = c_ref[...]
    row = lax.broadcasted_iota(jnp.int32, (tq, tq), 0)
    col = lax.broadcasted_iota(jnp.int32, (tq, tq), 1)
    causal = col <= row
    nt = (((1,), (1,)), ((), ()))

    for h in range(FOX_HEADS):
        lo, hi = h * FOX_HEAD_DIM, (h + 1) * FOX_HEAD_DIM
        q = q_all[:, lo:hi]
        cq = c_all[:, h:h + 1]

        def block(j, carry, masked):
            m, l, acc = carry
            start = pl.multiple_of(j * tq, tq)
            ks = kbf[pl.ds(start, tq), lo:hi]
            vs = vbf[pl.ds(start, tq), lo:hi]
            ck = ct_ref[j][h:h + 1, :]
            s = lax.dot_general(q, ks, nt, preferred_element_type=F32) + (cq - ck)
            if masked:
                s = jnp.where(causal, s, NEG)
            m_new = jnp.maximum(m, jnp.max(s, axis=-1, keepdims=True))
            p = jnp.exp(s - m_new)
            alpha = jnp.exp(m - m_new)
            l = alpha * l + jnp.sum(p, axis=-1, keepdims=True)
            acc = alpha * acc + jnp.dot(p.astype(BF16), vs, preferred_element_type=F32)
            return m_new, l, acc

        init = (jnp.full((tq, 1), NEG, F32), jnp.zeros((tq, 1), F32), jnp.zeros((tq, FOX_HEAD_DIM), F32))
        carry = lax.fori_loop(0, qi, lambda j, c: block(j, c, False), init)
        m, l, acc = block(qi, carry, True)
        o_ref[:, lo:hi] = acc / l


def _fox_prompt(h, c, ct, nb, seq):
    tq = FOX_CH
    nq = seq // tq
    t = nb * seq
    return pl.pallas_call(
        _fox_kernel,
        grid=(nb, nq),
        in_specs=[pl.BlockSpec((tq, GROUP_W), lambda b, i: (b * nq + i, COL_Q // GROUP_W)),
                  pl.BlockSpec((seq, GROUP_W), lambda b, i: (b, COL_K // GROUP_W)),
                  pl.BlockSpec((seq, GROUP_W), lambda b, i: (b, COL_V // GROUP_W)),
                  pl.BlockSpec((tq, LANES), lambda b, i: (b * nq + i, 0)),
                  pl.BlockSpec((None, nq, SUBLANES, tq), lambda b, i: (b, 0, 0, 0))],
        out_specs=pl.BlockSpec((tq, GROUP_W), lambda b, i: (b * nq + i, 0)),
        out_shape=jax.ShapeDtypeStruct((t, GROUP_W), F32),
        scratch_shapes=[pltpu.VMEM((seq, GROUP_W), BF16), pltpu.VMEM((seq, GROUP_W), BF16)],
        compiler_params=_cp(("arbitrary", "arbitrary")),
        name="fox_prompt",
    )(h, h, h, c, ct)


DEC_PPS = 8


def _fox_dec_kernel(pt_ref, q_ref, kn_ref, vn_ref, lfn_ref, *rest):
    del pt_ref
    pps = DEC_PPS
    k_refs = rest[:pps]
    v_refs = rest[pps:2 * pps]
    lf_refs = rest[2 * pps:3 * pps]
    o_ref = rest[3 * pps]
    m_s, l_s, acc_s, carry_s, qm_s = rest[3 * pps + 1:]
    b = pl.program_id(0)
    s = pl.program_id(1)
    ns = pl.num_programs(1)
    hrow = lax.broadcasted_iota(jnp.int32, (FOX_HEADS, GROUP_W), 0)
    hcol = lax.broadcasted_iota(jnp.int32, (FOX_HEADS, GROUP_W), 1) // FOX_HEAD_DIM
    headmask = hrow == hcol
    nt = (((1,), (1,)), ((), ()))

    @pl.when(s == 0)
    def _():
        qrow = q_ref[pl.ds(b, 1), :] * (FOX_HEAD_DIM ** -0.5)
        qm = jnp.where(headmask, jnp.broadcast_to(qrow, (FOX_HEADS, GROUP_W)), 0.0)
        qm_s[...] = qm.astype(BF16)
        kn = kn_ref[pl.ds(b, 1), :].astype(BF16).astype(F32)
        m_s[...] = jnp.sum(qm.astype(BF16).astype(F32) * kn, axis=-1, keepdims=True)
        l_s[...] = jnp.ones_like(l_s)
        acc_s[...] = jnp.broadcast_to(vn_ref[pl.ds(b, 1), :].astype(BF16).astype(F32), (FOX_HEADS, GROUP_W))
        lfn = jnp.broadcast_to(lfn_ref[pl.ds(b, 1), :], (FOX_HEADS, LANES))
        er = lax.broadcasted_iota(jnp.int32, (FOX_HEADS, LANES), 0)
        ec = lax.broadcasted_iota(jnp.int32, (FOX_HEADS, LANES), 1)
        carry_s[...] = jnp.sum(jnp.where(er == ec, lfn, 0.0), axis=-1, keepdims=True)

    r = lax.broadcasted_iota(jnp.int32, (PAGE_SIZE, PAGE_SIZE), 0)
    c = lax.broadcasted_iota(jnp.int32, (PAGE_SIZE, PAGE_SIZE), 1)
    strict = jnp.where(r > c, 1.0, 0.0).astype(BF16)
    qm = qm_s[...]
    m, l, acc, carry = m_s[...], l_s[...], acc_s[...], carry_s[...]
    for i in range(pps):
        kp = k_refs[i][...].astype(BF16)
        vp = v_refs[i][...].astype(BF16)
        lf = lf_refs[i][...]
        st = lax.dot_general(qm, kp, nt, preferred_element_type=F32)
        logits = st + (_dot01_right(lf, strict) + carry)
        carry = carry + jnp.sum(lf, axis=-1, keepdims=True)
        m_new = jnp.maximum(m, jnp.max(logits, axis=-1, keepdims=True))
        p = jnp.exp(logits - m_new)
        alpha = jnp.exp(m - m_new)
        l = alpha * l + jnp.sum(p, axis=-1, keepdims=True)
        acc = alpha * acc + jnp.dot(p.astype(BF16), vp, preferred_element_type=F32)
        m = m_new
    m_s[...], l_s[...], acc_s[...], carry_s[...] = m, l, acc, carry

    @pl.when(s == ns - 1)
    def _():
        o_ref[...] = jnp.sum(jnp.where(headmask, acc / l, 0.0), axis=0, keepdims=True)


def _fox_decode(hs, lfn, ck, cv, clft, page_table, layer):
    nb, npg = page_table.shape
    pps = DEC_PPS
    ns = npg // pps
    pt = page_table.reshape(-1)

    def page_map(i):
        return lambda b, s, pt_ref: (layer, pt_ref[b * npg + (npg - 1 - (s * pps + i))], 0, 0)

    full = lambda cb: pl.BlockSpec((nb, GROUP_W), lambda b, s, pt_ref: (0, cb))
    in_specs = [full(COL_Q // GROUP_W), full(COL_K // GROUP_W), full(COL_V // GROUP_W),
                pl.BlockSpec((nb, LANES), lambda b, s, pt_ref: (0, 0))]
    in_specs += [pl.BlockSpec((None, None, PAGE_SIZE, GROUP_W), page_map(i)) for i in range(pps)]
    in_specs += [pl.BlockSpec((None, None, PAGE_SIZE, GROUP_W), page_map(i)) for i in range(pps)]
    in_specs += [pl.BlockSpec((None, None, FOX_HEADS, PAGE_SIZE), page_map(i)) for i in range(pps)]
    out = pl.pallas_call(
        _fox_dec_kernel,
        grid_spec=pltpu.PrefetchScalarGridSpec(
            num_scalar_prefetch=1,
            grid=(nb, ns),
            in_specs=in_specs,
            out_specs=pl.BlockSpec((None, 1, GROUP_W), lambda b, s, pt_ref: (b, 0, 0)),
            scratch_shapes=[pltpu.VMEM((FOX_HEADS, 1), F32), pltpu.VMEM((FOX_HEADS, 1), F32),
                            pltpu.VMEM((FOX_HEADS, GROUP_W), F32), pltpu.VMEM((FOX_HEADS, 1), F32),
                            pltpu.VMEM((FOX_HEADS, GROUP_W), BF16)]),
        out_shape=jax.ShapeDtypeStruct((nb, 1, GROUP_W), F32),
        compiler_params=_cp(("arbitrary", "arbitrary")),
        name="fox_decode",
    )(pt, hs, hs, hs, lfn, *([ck] * pps), *([cv] * pps), *([clft] * pps))
    return out.reshape(nb, GROUP_W)


def _ssd_kernel(xbc_ref, z_ref, misc_ref, cw_ref, cb_ref, hp_ref, o_ref, st_ref, cs_ref, carry, hstate):
    q = SSD_CHUNK
    c = pl.program_id(1)
    nc = pl.num_programs(1)

    @pl.when(c == 0)
    def _():
        carry[...] = jnp.zeros_like(carry)
        hstate[...] = jnp.zeros_like(hstate)

    xin = xbc_ref[...]
    full = jnp.concatenate([carry[...], xin], axis=0)
    conv = cb_ref[...] + sum(full[SUBLANES - (SSD_CONV - 1) + w:SUBLANES - (SSD_CONV - 1) + w + q, :] * cw_ref[w:w + 1, :]
                             for w in range(SSD_CONV))
    carry[...] = xin[q - SUBLANES:q, :]
    cs_ref[...] = xin[q - SUBLANES:q, :]
    xc = _silu(conv)
    xs = xc[:, :GROUP_W]

    dt = _softplus(misc_ref[...] + hp_ref[0:1, :])
    a_row = -jnp.exp(hp_ref[1:2, :])
    d_row = hp_ref[2:3, :]
    da = dt * a_row
    row = lax.broadcasted_iota(jnp.int32, (q, q), 0)
    col = lax.broadcasted_iota(jnp.int32, (q, q), 1)
    tril = row >= col
    tril01 = jnp.where(tril, 1.0, 0.0).astype(BF16)
    triu01 = jnp.where(row <= col, 1.0, 0.0).astype(BF16)
    acs = _dot01_left(tril01, da)
    dt_t = dt.T
    acs_t = _dot01_right(da.T, triu01)
    nt = (((1,), (1,)), ((), ()))
    tn = (((0,), (0,)), ((), ()))
    rep = SSD_HEADS // SSD_GROUPS
    for g in range(SSD_GROUPS):
        bm = xc[:, GROUP_W + g * SSD_STATE:GROUP_W + (g + 1) * SSD_STATE].astype(BF16)
        cm = xc[:, GROUP_W + (SSD_GROUPS + g) * SSD_STATE:GROUP_W + (SSD_GROUPS + g + 1) * SSD_STATE].astype(BF16)
        cbm = lax.dot_general(cm, bm, nt, preferred_element_type=F32)
        for r in range(rep):
            hd = g * rep + r
            ln = MISC_DT + hd
            lo, hi = hd * SSD_HEAD_DIM, (hd + 1) * SSD_HEAD_DIM
            a_col = acs[:, ln:ln + 1]
            a_rw = acs_t[ln:ln + 1, :]
            dt_rw = dt_t[ln:ln + 1, :]
            dt_col = dt[:, ln:ln + 1]
            a_last = acs[q - 1:q, ln:ln + 1]
            decay = jnp.where(tril, jnp.exp(jnp.minimum(a_col - a_rw, 0.0)), 0.0)
            mm = (cbm * decay * dt_rw).astype(BF16)
            xh = xs[:, lo:hi]
            y = jnp.dot(mm, xh.astype(BF16), preferred_element_type=F32)
            xw = (xh * (jnp.exp(a_last - a_col) * dt_col)).astype(BF16)
            states = lax.dot_general(xw, bm, tn, preferred_element_type=F32)
            hprev = hstate[hd]
            y_off = lax.dot_general(cm, hprev.astype(BF16), nt, preferred_element_type=F32)
            y = y + y_off * jnp.exp(a_col) + d_row[:, ln:ln + 1] * xh
            hstate[hd] = hprev * jnp.exp(a_last) + states
            o_ref[:, lo:hi] = y
    o_ref[...] = o_ref[...] * _silu(z_ref[...])

    @pl.when(c == nc - 1)
    def _():
        st_ref[...] = hstate[...]


def _ssd_prompt(h, cw, cb, hp, nb, seq):
    q = SSD_CHUNK
    nc = seq // q
    t = nb * seq
    return pl.pallas_call(
        _ssd_kernel,
        grid=(nb, nc),
        in_specs=[pl.BlockSpec((q, SSD_CONV_DIM), lambda b, c: (b * nc + c, COL_XBC // SSD_CONV_DIM)),
                  pl.BlockSpec((q, GROUP_W), lambda b, c: (b * nc + c, COL_Z // GROUP_W)),
                  pl.BlockSpec((q, LANES), lambda b, c: (b * nc + c, COL_MISC // LANES)),
                  pl.BlockSpec((SSD_CONV, SSD_CONV_DIM), lambda b, c: (0, 0)),
                  pl.BlockSpec((1, SSD_CONV_DIM), lambda b, c: (0, 0)),
                  pl.BlockSpec((SUBLANES, LANES), lambda b, c: (0, 0))],
        out_specs=[pl.BlockSpec((q, GROUP_W), lambda b, c: (b * nc + c, 0)),
                   pl.BlockSpec((None, SSD_HEADS, SSD_HEAD_DIM, SSD_STATE), lambda b, c: (b, 0, 0, 0)),
                   pl.BlockSpec((None, SUBLANES, SSD_CONV_DIM), lambda b, c: (b, 0, 0))],
        out_shape=[jax.ShapeDtypeStruct((t, GROUP_W), F32),
                   jax.ShapeDtypeStruct((nb, SSD_HEADS, SSD_HEAD_DIM, SSD_STATE), F32),
                   jax.ShapeDtypeStruct((nb, SUBLANES, SSD_CONV_DIM), F32)],
        scratch_shapes=[pltpu.VMEM((SUBLANES, SSD_CONV_DIM), F32),
                        pltpu.VMEM((SSD_HEADS, SSD_HEAD_DIM, SSD_STATE), F32)],
        compiler_params=_cp(("arbitrary", "arbitrary")),
        name="ssd_prompt",
    )(h, h, h, cw, cb, hp)


def _gmlp_kernel(u_ref, v_ref, lng_ref, lnb_ref, ws_ref, bst_ref, o_ref):
    n = GMLP_CHUNK
    v = _layer_norm(_gelu(v_ref[...]), lng_ref[...], lnb_ref[...])
    row = lax.broadcasted_iota(jnp.int32, (n, n), 0)
    col = lax.broadcasted_iota(jnp.int32, (n, n), 1)
    tril = row >= col
    gc = GROUP_W // GMLP_GROUPS
    parts = []
    for g in range(GMLP_GROUPS):
        wm = jnp.where(tril, ws_ref[g], 0.0).astype(BF16)
        sg = jnp.dot(wm, v[:, g * gc:(g + 1) * gc].astype(BF16), preferred_element_type=F32)
        parts.append(sg + bst_ref[:, g:g + 1])
    s = jnp.concatenate(parts, axis=-1)
    o_ref[...] = _gelu(u_ref[...]) * s


def _gmlp_prompt(h, lng, lnb, ws, bst, t):
    n = GMLP_CHUNK
    return pl.pallas_call(
        _gmlp_kernel,
        grid=(t // n,),
        in_specs=[pl.BlockSpec((n, GROUP_W), lambda i: (i, COL_GU // GROUP_W)),
                  pl.BlockSpec((n, GROUP_W), lambda i: (i, COL_GV // GROUP_W)),
                  pl.BlockSpec((1, GROUP_W), lambda i: (0, 0)),
                  pl.BlockSpec((1, GROUP_W), lambda i: (0, 0)),
                  pl.BlockSpec((GMLP_GROUPS, n, n), lambda i: (0, 0, 0)),
                  pl.BlockSpec((n, LANES), lambda i: (0, 0))],
        out_specs=pl.BlockSpec((n, GROUP_W), lambda i: (i, 0)),
        out_shape=jax.ShapeDtypeStruct((t, GROUP_W), F32),
        compiler_params=_cp(("arbitrary",)),
        name="gmlp_prompt",
    )(h, h, lng, lnb, ws, bst)


CONF_HALO = 32
CONF_CHUNK = 128


def _conf_kernel(a_ref, g_ref, cw_ref, cb_ref, lng_ref, lnb_ref, o_ref, st_ref, carry):
    n = CONF_CHUNK

    @pl.when(pl.program_id(1) == 0)
    def _():
        carry[...] = jnp.zeros_like(carry)

    glu = a_ref[...] * _sigmoid(g_ref[...])
    full = jnp.concatenate([carry[...], glu], axis=0)
    off = CONF_HALO - (CONF_KERNEL - 1)
    acc = cb_ref[...] + full[off:off + n, :] * cw_ref[0:1, :]
    for w in range(1, CONF_KERNEL):
        acc = acc + full[off + w:off + w + n, :] * cw_ref[w:w + 1, :]
    carry[...] = glu[n - CONF_HALO:n, :]
    st_ref[...] = glu[n - CONF_HALO:n, :]
    o_ref[...] = _silu(_layer_norm(acc, lng_ref[...], lnb_ref[...]))


def _conf_prompt(h, cw, cb, lng, lnb, nb, seq):
    n = CONF_CHUNK
    nc = seq // n
    t = nb * seq
    return pl.pallas_call(
        _conf_kernel,
        grid=(nb, nc),
        in_specs=[pl.BlockSpec((n, GROUP_W), lambda b, c: (b * nc + c, COL_GA // GROUP_W)),
                  pl.BlockSpec((n, GROUP_W), lambda b, c: (b * nc + c, COL_GB // GROUP_W)),
                  pl.BlockSpec((CONF_HALO, GROUP_W), lambda b, c: (0, 0)),
                  pl.BlockSpec((1, GROUP_W), lambda b, c: (0, 0)),
                  pl.BlockSpec((1, GROUP_W), lambda b, c: (0, 0)),
                  pl.BlockSpec((1, GROUP_W), lambda b, c: (0, 0))],
        out_specs=[pl.BlockSpec((n, GROUP_W), lambda b, c: (b * nc + c, 0)),
                   pl.BlockSpec((None, CONF_HALO, GROUP_W), lambda b, c: (b, 0, 0))],
        out_shape=[jax.ShapeDtypeStruct((t, GROUP_W), F32),
                   jax.ShapeDtypeStruct((nb, CONF_HALO, GROUP_W), F32)],
        scratch_shapes=[pltpu.VMEM((CONF_HALO, GROUP_W), F32)],
        compiler_params=_cp(("arbitrary", "arbitrary")),
        name="conf_prompt",
    )(h, h, cw, cb, lng, lnb)


def _dec_rows_kernel(gu_ref, gv_ref, ga_ref, gb_ref, xbc_ref, misc_ref,
                     lng_ref, lnb_ref, w00_ref, b0_ref,
                     ccw_ref, ccb_ref, clg_ref, clb_ref, cprev_ref,
                     scw_ref, scb_ref, sprev_ref, hp_ref,
                     c_ref, vrow_ref, d_ref, glu_ref, xc_ref, sm_ref):
    nb = gu_ref.shape[0]
    v = _layer_norm(_gelu(gv_ref[...]), lng_ref[...], lnb_ref[...])
    vrow_ref[...] = v
    c_ref[...] = _gelu(gu_ref[...]) * (w00_ref[...] * v + b0_ref[...])
    glu = ga_ref[...] * _sigmoid(gb_ref[...])
    glu_ref[...] = glu
    kw = CONF_KERNEL - 1
    rows = []
    for b in range(nb):
        rows.append(jnp.sum(cprev_ref[b] * ccw_ref[0:kw, :], axis=0, keepdims=True))
    conv = jnp.concatenate(rows, axis=0) + glu * ccw_ref[kw:kw + 1, :] + ccb_ref[...]
    d_ref[...] = _silu(_layer_norm(conv, clg_ref[...], clb_ref[...]))
    sk = SSD_CONV - 1
    rows = []
    for b in range(nb):
        rows.append(jnp.sum(sprev_ref[b] * scw_ref[0:sk, :], axis=0, keepdims=True))
    sconv = jnp.concatenate(rows, axis=0) + xbc_ref[...] * scw_ref[sk:sk + 1, :] + scb_ref[...]
    xc_ref[...] = _silu(sconv)
    misc = misc_ref[...]
    lane = lax.broadcasted_iota(jnp.int32, misc.shape, 1)
    lf = _log_sigmoid(misc + hp_ref[3:4, :])
    dt = _softplus(misc + hp_ref[0:1, :])
    sm_ref[...] = jnp.where(lane < MISC_DT, lf, dt)


def _dec_rows(hs, lng, lnb, w00, b0, ccw, ccb, clg, clb, cprev, scw, scb, sprev, hp):
    nb = hs.shape[0]
    col = lambda c0, w: pl.BlockSpec((nb, w), lambda i: (0, c0 // w))
    whole = lambda a: pl.BlockSpec(a.shape, lambda i: (0,) * a.ndim)
    sd = lambda w: jax.ShapeDtypeStruct((nb, w), F32)
    return pl.pallas_call(
        _dec_rows_kernel,
        grid=(1,),
        in_specs=[col(COL_GU, GROUP_W), col(COL_GV, GROUP_W), col(COL_GA, GROUP_W), col(COL_GB, GROUP_W),
                  col(COL_XBC, SSD_CONV_DIM), col(COL_MISC, LANES)]
                 + [whole(a) for a in (lng, lnb, w00, b0, ccw, ccb, clg, clb, cprev, scw, scb, sprev, hp)],
        out_specs=[pl.BlockSpec((nb, GROUP_W), lambda i: (0, 0))] * 4
                  + [pl.BlockSpec((nb, SSD_CONV_DIM), lambda i: (0, 0)), pl.BlockSpec((nb, LANES), lambda i: (0, 0))],
        out_shape=[sd(GROUP_W)] * 4 + [sd(SSD_CONV_DIM), sd(LANES)],
        compiler_params=_cp(("arbitrary",)),
        name="dec_rows",
    )(hs, hs, hs, hs, hs, hs, lng, lnb, w00, b0, ccw, ccb, clg, clb, cprev, scw, scb, sprev, hp)


def _dec_ssm_kernel(xt_ref, zt_ref, bc_ref, sm_ref, hp_ref, h_ref, yt_ref, hn_ref):
    nb = bc_ref.shape[0]
    rep = SSD_HEADS // SSD_GROUPS
    a_row = -jnp.exp(hp_ref[1:2, :])
    d_row = hp_ref[2:3, :]
    sm = sm_ref[...]
    for b in range(nb):
        cols = []
        for hd in range(SSD_HEADS):
            g = hd // rep
            ln = MISC_DT + hd
            lo, hi = hd * SSD_HEAD_DIM, (hd + 1) * SSD_HEAD_DIM
            dt = sm[b:b + 1, ln:ln + 1]
            x = xt_ref[lo:hi, b:b + 1]
            brow = bc_ref[b:b + 1, g * SSD_STATE:(g + 1) * SSD_STATE]
            crow = bc_ref[b:b + 1, (SSD_GROUPS + g) * SSD_STATE:(SSD_GROUPS + g + 1) * SSD_STATE]
            hnew = h_ref[b, hd] * jnp.exp(dt * a_row[:, ln:ln + 1]) + (dt * x) * brow
            hn_ref[b, hd] = hnew
            y = jnp.sum(hnew.astype(BF16).astype(F32) * crow.astype(BF16).astype(F32), axis=-1, keepdims=True)
            cols.append(y + d_row[:, ln:ln + 1] * x)
        ycol = jnp.concatenate(cols, axis=0)
        yt_ref[:, b:b + 1] = ycol * _silu(zt_ref[:, b:b + 1])


def _dec_ssm(xt, zt, bc, sm, hp, hstate):
    nb = bc.shape[0]
    whole = lambda a: pl.BlockSpec(a.shape, lambda i: (0,) * a.ndim)
    return pl.pallas_call(
        _dec_ssm_kernel,
        grid=(1,),
        in_specs=[whole(a) for a in (xt, zt, bc, sm, hp, hstate)],
        out_specs=[pl.BlockSpec((GROUP_W, nb), lambda i: (0, 0)),
                   pl.BlockSpec(hstate.shape, lambda i: (0, 0, 0, 0))],
        out_shape=[jax.ShapeDtypeStruct((GROUP_W, nb), F32), jax.ShapeDtypeStruct(hstate.shape, F32)],
        compiler_params=_cp(("arbitrary",)),
        name="dec_ssm",
    )(xt, zt, bc, sm, hp, hstate)


def _outproj_kernel(a_ref, b_ref, c_ref, d_ref, x_ref, w_ref, mg_ref, g_ref, bb_ref, rw_ref, rb_ref,
                    h_ref, hb_ref, ti_ref, tg_ref):
    y = None
    for i, m_ref in enumerate((a_ref, b_ref, c_ref, d_ref)):
        m = m_ref[...]
        r = lax.rsqrt(jnp.mean(m * m, axis=-1, keepdims=True) + EPS)
        nrm = (m * r * mg_ref[:, i * GROUP_W:(i + 1) * GROUP_W]).astype(BF16)
        part = jnp.dot(nrm, w_ref[i * GROUP_W:(i + 1) * GROUP_W, :], preferred_element_type=F32)
        y = part if y is None else y + part
    h = _layer_norm(DEEPNORM_ALPHA * x_ref[...] + y, g_ref[...], bb_ref[...])
    h_ref[...] = h
    hb_ref[...] = h.astype(BF16)
    hh = h.astype(BF16)
    hl = (h - hh.astype(F32)).astype(BF16)
    rw = rw_ref[...]
    wh = rw.astype(BF16)
    wl = (rw - wh.astype(F32)).astype(BF16)
    d = lambda p, q: jnp.dot(p, q, preferred_element_type=F32)
    lg = d(hh, wh) + d(hh, wl) + d(hl, wh) + rb_ref[...]
    lane = lax.broadcasted_iota(jnp.int32, lg.shape, 1)
    ti = jnp.zeros(lg.shape, jnp.int32)
    tv = jnp.full(lg.shape, NEG, F32)
    for k in range(TOP_K):
        mx = jnp.max(lg, axis=-1, keepdims=True)
        idx = jnp.min(jnp.where(lg == mx, lane, LANES), axis=-1, keepdims=True)
        ti = jnp.where(lane == k, idx, ti)
        tv = jnp.where(lane == k, mx, tv)
        lg = jnp.where(lane == idx, NEG, lg)
    e = jnp.exp(tv - jnp.max(tv, axis=-1, keepdims=True))
    tg_ref[...] = e / jnp.sum(e, axis=-1, keepdims=True)
    ti_ref[...] = ti


def _outproj(a, b, c, d, x, w, mg, g, bb, rw, rb, tm):
    t = x.shape[0]
    row = lambda w_: pl.BlockSpec((tm, w_), lambda i: (i, 0))
    whole = lambda arr: pl.BlockSpec(arr.shape, lambda i: (0,) * arr.ndim)
    return pl.pallas_call(
        _outproj_kernel,
        grid=(pl.cdiv(t, tm),),
        in_specs=[row(GROUP_W)] * 4 + [row(D_MODEL)] + [whole(arr) for arr in (w, mg, g, bb, rw, rb)],
        out_specs=[row(D_MODEL), row(D_MODEL), row(LANES), row(LANES)],
        out_shape=[jax.ShapeDtypeStruct((t, D_MODEL), F32), jax.ShapeDtypeStruct((t, D_MODEL), BF16),
                   jax.ShapeDtypeStruct((t, LANES), jnp.int32), jax.ShapeDtypeStruct((t, LANES), F32)],
        compiler_params=_cp(("arbitrary",)),
        name="outproj_ln_router",
    )(a, b, c, d, x, w, mg, g, bb, rw, rb)


def _moe_kernel(ge_ref, gb_ref, gr_ref, x_ref, wg_ref, wu_ref, wd_ref, bg_ref, bu_ref, bd_ref, y_ref,
                wgb, wub, wdb):
    del ge_ref, gb_ref
    g = pl.program_id(0)
    f = pl.program_id(1)
    rows = gr_ref[g]

    @pl.when(rows > 0)
    def _():
        wgb[...] = wg_ref[...].astype(BF16)
        wub[...] = wu_ref[...].astype(BF16)
        wdb[...] = wd_ref[...].astype(BF16)

    for sb in range(MOE_R // MOE_SB):
        @pl.when(sb * MOE_SB < rows)
        def _():
            sl = pl.ds(sb * MOE_SB, MOE_SB)
            xb = x_ref[sl, :]
            hg = jnp.dot(xb, wgb[...], preferred_element_type=F32) + bg_ref[...]
            hu = jnp.dot(xb, wub[...], preferred_element_type=F32) + bu_ref[...]
            gate = jnp.minimum(hg, SWIGLU_LIMIT)
            up = jnp.clip(hu, -SWIGLU_LIMIT, SWIGLU_LIMIT)
            act = (up + 1.0) * (gate * _sigmoid(SWIGLU_ALPHA * gate))
            part = jnp.dot(act.astype(BF16), wdb[...], preferred_element_type=F32)

            @pl.when(f == 0)
            def _():
                y_ref[sl, :] = part + bd_ref[...]

            @pl.when(f > 0)
            def _():
                y_ref[sl, :] = y_ref[sl, :] + part


def _moe_ffn(x_rows, w_gu, b_gu, w_dn, b_dn, ge, gb, gr):
    ng = ge.shape[0]
    d = x_rows.shape[1]
    nf = D_FF // MOE_TF
    r = MOE_R

    def fidx(g, f, gr_ref):
        return jnp.where(gr_ref[g] > 0, f, nf - 1)

    in_specs = [
        pl.BlockSpec((r, d), lambda g, f, ge_, gb_, gr_: (gb_[g], 0)),
        pl.BlockSpec((None, d, MOE_TF), lambda g, f, ge_, gb_, gr_: (ge_[g], 0, fidx(g, f, gr_))),
        pl.BlockSpec((None, d, MOE_TF), lambda g, f, ge_, gb_, gr_: (ge_[g], 0, nf + fidx(g, f, gr_))),
        pl.BlockSpec((None, MOE_TF, d), lambda g, f, ge_, gb_, gr_: (ge_[g], fidx(g, f, gr_), 0)),
        pl.BlockSpec((None, 1, MOE_TF), lambda g, f, ge_, gb_, gr_: (ge_[g], 0, fidx(g, f, gr_))),
        pl.BlockSpec((None, 1, MOE_TF), lambda g, f, ge_, gb_, gr_: (ge_[g], 0, nf + fidx(g, f, gr_))),
        pl.BlockSpec((None, 1, d), lambda g, f, ge_, gb_, gr_: (ge_[g], 0, 0)),
    ]
    return pl.pallas_call(
        _moe_kernel,
        grid_spec=pltpu.PrefetchScalarGridSpec(
            num_scalar_prefetch=3,
            grid=(ng, nf),
            in_specs=in_specs,
            out_specs=pl.BlockSpec((r, d), lambda g, f, ge_, gb_, gr_: (gb_[g], 0)),
            scratch_shapes=[pltpu.VMEM((d, MOE_TF), BF16), pltpu.VMEM((d, MOE_TF), BF16),
                            pltpu.VMEM((MOE_TF, d), BF16)]),
        out_shape=jax.ShapeDtypeStruct((ng * r, d), F32),
        compiler_params=_cp(("arbitrary", "arbitrary")),
        name="moe_ffn",
    )(ge, gb, gr, x_rows, w_gu, w_gu, w_dn, b_gu.reshape(N_EXPERTS, 1, -1), b_gu.reshape(N_EXPERTS, 1, -1),
      b_dn.reshape(N_EXPERTS, 1, -1))


def _moe_plan(top_i, n_tok):
    n_pairs = n_tok * TOP_K
    r = MOE_R
    ng = n_pairs // r + N_EXPERTS
    flat_e = top_i.reshape(-1)
    order = jnp.argsort(flat_e)
    sorted_e = flat_e[order]
    sorted_tok = (order // TOP_K).astype(jnp.int32)
    counts = jnp.bincount(flat_e, length=N_EXPERTS).astype(jnp.int32)
    ngrp = (counts + r - 1) // r
    grp_end = jnp.cumsum(ngrp)
    grp_start = grp_end - ngrp
    start = jnp.cumsum(counts) - counts
    dest_sorted = grp_start[sorted_e] * r + jnp.arange(n_pairs, dtype=jnp.int32) - start[sorted_e]
    row_tok = jnp.full((ng * r,), n_tok, jnp.int32).at[dest_sorted].set(sorted_tok)
    dest = jnp.zeros((n_pairs,), jnp.int32).at[order].set(dest_sorted)
    total = grp_end[-1]
    gidx = jnp.arange(ng, dtype=jnp.int32)
    valid = gidx < total
    gclamp = jnp.minimum(gidx, total - 1)
    ge = jnp.sum(grp_end[None, :] <= gclamp[:, None], axis=1).astype(jnp.int32)
    local = gclamp - grp_start[ge]
    gr = jnp.where(valid, jnp.clip(counts[ge] - local * r, 0, r), 0).astype(jnp.int32)
    return row_tok, dest.reshape(n_tok, TOP_K), ge, gclamp.astype(jnp.int32), gr


def _combine_kernel(y_ref, tg_ref, h_ref, g_ref, b_ref, o_ref):
    d = h_ref.shape[1]
    tg = tg_ref[...]
    ffn = tg[:, 0:1] * y_ref[:, 0:d]
    for k in range(1, TOP_K):
        ffn = ffn + tg[:, k:k + 1] * y_ref[:, k * d:(k + 1) * d]
    o_ref[...] = _layer_norm(DEEPNORM_ALPHA * h_ref[...] + ffn, g_ref[...], b_ref[...])


def _combine(yg, tg, h, g, b, tm):
    t, d = h.shape
    return pl.pallas_call(
        _combine_kernel,
        grid=(pl.cdiv(t, tm),),
        in_specs=[pl.BlockSpec((tm, TOP_K * d), lambda i: (i, 0)),
                  pl.BlockSpec((tm, LANES), lambda i: (i, 0)),
                  pl.BlockSpec((tm, d), lambda i: (i, 0)),
                  pl.BlockSpec((1, d), lambda i: (0, 0)),
                  pl.BlockSpec((1, d), lambda i: (0, 0))],
        out_specs=pl.BlockSpec((tm, d), lambda i: (i, 0)),
        out_shape=jax.ShapeDtypeStruct((t, d), F32),
        compiler_params=_cp(("arbitrary",)),
        name="combine_ln2",
    )(yg, tg, h, g, b)


def _lane_row(vals, offset, width=LANES):
    row = jnp.zeros((width,), F32)
    return lax.dynamic_update_slice(row, vals.astype(F32), (offset,)).reshape(1, width)


def _layer(xp, xs, P, past, nb, seq):
    tp = xp.shape[0]
    nd = xs.shape[0]
    w_in = _prep_w_in(P['w_in'])
    hp_ = _matmul(xp, w_in, 512, IN_TN)
    hs_ = _matmul(xs, w_in, nd, IN_TN)

    hp = jnp.concatenate([_lane_row(P['ssd_dt_bias'], MISC_DT), _lane_row(P['ssd_a_log'], MISC_DT),
                          _lane_row(P['ssd_d'], MISC_DT), _lane_row(P['b_forget'], MISC_F),
                          jnp.zeros((4, LANES), F32)], axis=0)
    bf_row = hp[3:4]
    r1 = lambda a: a.reshape(1, -1).astype(F32)

    logf, c, ct = _fox_prep(hp_, bf_row, nb, seq)
    a_out = _fox_prompt(hp_, c, ct, nb, seq)
    b_out, ssm_p, sconv_p = _ssd_prompt(hp_, P['ssd_conv_w'], r1(P['ssd_conv_b']), hp, nb, seq)
    bst = jnp.pad(P['gmlp_bs'].T, ((0, 0), (0, LANES - GMLP_GROUPS)))
    c_out = _gmlp_prompt(hp_, r1(P['gmlp_ln_g']), r1(P['gmlp_ln_b']), P['gmlp_ws'], bst, tp)
    ccw = jnp.pad(P['conf_conv_w'], ((0, CONF_HALO - CONF_KERNEL), (0, 0)))
    d_out, conf_p = _conf_prompt(hp_, ccw, r1(P['conf_conv_b']), r1(P['conf_ln_g']), r1(P['conf_ln_b']), nb, seq)

    k_p = hp_[:tp, COL_K:COL_K + GROUP_W].reshape(nb, seq, FOX_HEADS, FOX_HEAD_DIM)
    v_p = hp_[:tp, COL_V:COL_V + GROUP_W].reshape(nb, seq, FOX_HEADS, FOX_HEAD_DIM)
    logf_p = logf[:, :FOX_HEADS].reshape(nb, seq, FOX_HEADS)
    state_p = (k_p, v_p, logf_p, ssm_p, sconv_p[:, SUBLANES - (SSD_CONV - 1):, :],
               conf_p[:, CONF_HALO - (CONF_KERNEL - 1):, :])

    w00 = jnp.repeat(P['gmlp_ws'][:, 0, 0], GROUP_W // GMLP_GROUPS).reshape(1, GROUP_W)
    b0 = jnp.repeat(P['gmlp_bs'][:, 0], GROUP_W // GMLP_GROUPS).reshape(1, GROUP_W)
    scw = jnp.pad(P['ssd_conv_w'], ((0, SUBLANES - SSD_CONV), (0, 0)))
    c_s, vrow_s, d_s, glu_s, xc_s, sm_s = _dec_rows(
        hs_, r1(P['gmlp_ln_g']), r1(P['gmlp_ln_b']), w00, b0,
        ccw, r1(P['conf_conv_b']), r1(P['conf_ln_g']), r1(P['conf_ln_b']), past['state_conv'],
        scw, r1(P['ssd_conv_b']), past['state_ssd_conv'], hp)
    yt, ssm_s = _dec_ssm(xc_s[:, :GROUP_W].T, hs_[:, COL_Z:COL_Z + GROUP_W].T, xc_s[:, GROUP_W:], sm_s, hp,
                         past['state_ssm'])
    b_s = yt.T
    a_s = _fox_decode(hs_, sm_s, past['cache_k'], past['cache_v'], past['cache_lft'], past['page_table'],
                      past['layer'])
    k_s = hs_[:, COL_K:COL_K + GROUP_W].reshape(nd, 1, FOX_HEADS, FOX_HEAD_DIM)
    v_s = hs_[:, COL_V:COL_V + GROUP_W].reshape(nd, 1, FOX_HEADS, FOX_HEAD_DIM)
    logf_s = sm_s[:, :FOX_HEADS].reshape(nd, 1, FOX_HEADS)
    sconv_s = jnp.concatenate([past['state_ssd_conv'][:, 1:], hs_[:, None, COL_XBC:COL_XBC + SSD_CONV_DIM]], axis=1)
    conf_s = jnp.concatenate([past['state_conv'][:, 1:], glu_s[:, None, :]], axis=1)
    state_s = (k_s, v_s, logf_s, ssm_s, sconv_s, conf_s, vrow_s.reshape(nd, 1, GROUP_W))

    w_out = P['w_out'].astype(BF16)
    rw = jnp.pad(P['router_w'], ((0, 0), (0, LANES - N_EXPERTS)))
    rb = jnp.concatenate([P['router_b'].astype(F32), jnp.full((LANES - N_EXPERTS,), NEG, F32)]).reshape(1, LANES)
    op_args = (w_out, r1(P['mix_norm_g']), r1(P['ln1_g']), r1(P['ln1_b']), rw, rb)
    h_p, hb_p, ti_p, tg_p = _outproj(a_out, b_out, c_out, d_out, xp, *op_args, tm=256)
    h_s, hb_s, ti_s, tg_s = _outproj(a_s, b_s, c_s, d_s, xs, *op_args, tm=nd)

    n_tok = tp + nd
    top_i = jnp.concatenate([ti_p[:, :TOP_K], ti_s[:, :TOP_K]], axis=0)
    row_tok, dest, ge, gb, gr = _moe_plan(top_i, n_tok)
    hb_all = jnp.concatenate([hb_p, hb_s, jnp.zeros((1, D_MODEL), BF16)], axis=0)
    x_rows = hb_all[row_tok]
    y_rows = _moe_ffn(x_rows, P['exp_w_gu'], P['exp_b_gu'], P['exp_w_dn'], P['exp_b_dn'], ge, gb, gr)
    yg = y_rows[dest].reshape(n_tok, TOP_K * D_MODEL)
    o_p = _combine(yg[:tp], tg_p, h_p, r1(P['ln2_g']), r1(P['ln2_b']), tm=256)
    o_s = _combine(yg[tp:], tg_s, h_s, r1(P['ln2_g']), r1(P['ln2_b']), tm=nd)
    return o_p, o_s, state_p, state_s


PARAM_NAMES = ('w_in', 'b_forget', 'ssd_conv_w', 'ssd_conv_b', 'ssd_dt_bias', 'ssd_a_log', 'ssd_d', 'gmlp_ln_g',
               'gmlp_ln_b', 'gmlp_ws', 'gmlp_bs', 'conf_conv_w', 'conf_conv_b', 'conf_ln_g', 'conf_ln_b',
               'mix_norm_g', 'w_out', 'ln1_g', 'ln1_b', 'router_w', 'router_b', 'exp_w_gu', 'exp_b_gu',
               'exp_w_dn', 'exp_b_dn', 'ln2_g', 'ln2_b')


def kernel(x_prompt, x_sample, cache_k, cache_v, cache_logf, state_ssm, state_ssd_conv, state_conv, page_table, w_in, b_forget, ssd_conv_w, ssd_conv_b, ssd_dt_bias, ssd_a_log, ssd_d, gmlp_ln_g, gmlp_ln_b, gmlp_ws, gmlp_bs, conf_conv_w, conf_conv_b, conf_ln_g, conf_ln_b, mix_norm_g, w_out, ln1_g, ln1_b, router_w, router_b, exp_w_gu, exp_b_gu, exp_w_dn, exp_b_dn, ln2_g, ln2_b):
    weights = (w_in, b_forget, ssd_conv_w, ssd_conv_b, ssd_dt_bias, ssd_a_log, ssd_d, gmlp_ln_g, gmlp_ln_b, gmlp_ws,
               gmlp_bs, conf_conv_w, conf_conv_b, conf_ln_g, conf_ln_b, mix_norm_g, w_out, ln1_g, ln1_b, router_w,
               router_b, exp_w_gu, exp_b_gu, exp_w_dn, exp_b_dn, ln2_g, ln2_b)
    nb, seq, d = x_prompt.shape
    nd = x_sample.shape[0]
    depth = w_in.shape[0]
    n_pool = cache_k.shape[1]
    ck = cache_k.reshape(depth, n_pool, PAGE_SIZE, GROUP_W)
    cv = cache_v.reshape(depth, n_pool, PAGE_SIZE, GROUP_W)
    clft = jnp.swapaxes(cache_logf, 2, 3)
    xp = x_prompt.reshape(nb * seq, d)
    xs = x_sample.reshape(nd, d)
    st_p, st_s = [], []
    for l in range(depth):
        P = {n: w[l] for n, w in zip(PARAM_NAMES, weights)}
        past = dict(cache_k=ck, cache_v=cv, cache_lft=clft, layer=l, state_ssm=state_ssm[l],
                    state_ssd_conv=state_ssd_conv[l], state_conv=state_conv[l], page_table=page_table)
        xp, xs, sp, ss = _layer(xp, xs, P, past, nb, seq)
        st_p.append(sp)
        st_s.append(ss)
    stack = lambda sts, i: jnp.stack([s[i] for s in sts], axis=0)
    return (xp.reshape(nb, seq, d), xs.reshape(nd, 1, d),
            stack(st_p, 0), stack(st_p, 1), stack(st_p, 2),
            stack(st_s, 0), stack(st_s, 1), stack(st_s, 2),
            stack(st_p, 3), stack(st_s, 3), stack(st_p, 4), stack(st_s, 4),
            stack(st_p, 5), stack(st_s, 5), stack(st_s, 6))
```

```python
import functools
import math

import jax
import jax.numpy as jnp
from jax import lax
from jax.experimental import pallas as pl
from jax.experimental.pallas import tpu as pltpu

F32 = jnp.float32
BF16 = jnp.bfloat16

D_MODEL = 2048
GROUP_W = 512
N_MIXERS = 4
FOX_HEADS = 8
FOX_HEAD_DIM = 64
SSD_HEADS = 8
SSD_HEAD_DIM = 64
SSD_STATE = 128
SSD_GROUPS = 2
SSD_CONV = 4
SSD_CHUNK = 128
SSD_CONV_DIM = GROUP_W + 2 * SSD_GROUPS * SSD_STATE
GMLP_CHUNK = 128
GMLP_GROUPS = 4
CONF_KERNEL = 31
IN_SIZES = (GROUP_W, GROUP_W, GROUP_W, FOX_HEADS, GROUP_W, SSD_CONV_DIM, SSD_HEADS,
            GROUP_W, GROUP_W, GROUP_W, GROUP_W)
IN_SPLITS = tuple(sum(IN_SIZES[:i + 1]) for i in range(len(IN_SIZES) - 1))
N_EXPERTS = 32
TOP_K = 4
D_FF = D_MODEL
SWIGLU_LIMIT = 7.0
SWIGLU_ALPHA = 1.702
DEPTH = 2
DEEPNORM_ALPHA = (2 * DEPTH) ** 0.25
EPS = 1e-5
PAGE_SIZE = 128

LANES = 128
SUBLANES = 8
VMEM_LIMIT = 56 * 1024 * 1024

COL_Q, COL_K, COL_V, COL_Z, COL_XBC, COL_GU, COL_GV, COL_GA, COL_GB, COL_MISC = (
    0, 512, 1024, 1536, 2048, 3072, 3584, 4096, 4608, 5120)
MISC_F = 0
MISC_DT = 8
IN_COLS_PAD = 5376
IN_TN = 1792
NEG = -1e30

MOE_R = 1152
MOE_SB = 384
MOE_TF = 256


def _cp(sem, vmem=VMEM_LIMIT):
    return pltpu.CompilerParams(dimension_semantics=sem, vmem_limit_bytes=vmem)


def _split3(x):
    hi = x.astype(BF16)
    r1 = x - hi.astype(F32)
    mid = r1.astype(BF16)
    lo = (r1 - mid.astype(F32)).astype(BF16)
    return hi, mid, lo


def _dot01_left(t01, x):
    hi, mid, lo = _split3(x)
    d = lambda b: jnp.dot(t01, b, preferred_element_type=F32)
    return d(hi) + d(mid) + d(lo)


def _dot01_right(x, t01):
    hi, mid, lo = _split3(x)
    d = lambda a: jnp.dot(a, t01, preferred_element_type=F32)
    return d(hi) + d(mid) + d(lo)


def _log_sigmoid(x):
    return jnp.minimum(x, 0.0) - jnp.log(1.0 + jnp.exp(-jnp.abs(x)))


def _softplus(x):
    return jnp.maximum(x, 0.0) + jnp.log(1.0 + jnp.exp(-jnp.abs(x)))


def _sigmoid(x):
    return 1.0 / (1.0 + jnp.exp(-x))


def _silu(x):
    return x * _sigmoid(x)


def _gelu(x):
    return 0.5 * x * (1.0 + jnp.tanh(math.sqrt(2.0 / math.pi) * (x + 0.044715 * (x * x * x))))


def _layer_norm(x, g, b):
    mu = jnp.mean(x, axis=-1, keepdims=True)
    xc = x - mu
    var = jnp.mean(xc * xc, axis=-1, keepdims=True)
    return xc * lax.rsqrt(var + EPS) * g + b


def _mm_kernel(x_ref, w_ref, o_ref):
    o_ref[...] = jnp.dot(x_ref[...].astype(BF16), w_ref[...], preferred_element_type=F32)


def _matmul(x, w, tm, tn):
    m, k = x.shape
    n = w.shape[1]
    return pl.pallas_call(
        _mm_kernel,
        grid=(n // tn, pl.cdiv(m, tm)),
        in_specs=[pl.BlockSpec((tm, k), lambda j, i: (i, 0)),
                  pl.BlockSpec((k, tn), lambda j, i: (0, j))],
        out_specs=pl.BlockSpec((tm, tn), lambda j, i: (i, j)),
        out_shape=jax.ShapeDtypeStruct((m, n), F32),
        compiler_params=_cp(("arbitrary", "arbitrary")),
        name="in_proj",
    )(x, w)


def _prep_w_in(w):
    q, k, v, f, z, xbc, dt, gu, gv, ga, gb = jnp.split(w, IN_SPLITS, axis=-1)
    d = w.shape[0]
    misc = jnp.concatenate([f, dt, jnp.zeros((d, LANES - 16), w.dtype)], axis=-1)
    pad = jnp.zeros((d, IN_COLS_PAD - (COL_MISC + LANES)), w.dtype)
    return jnp.concatenate([q, k, v, z, xbc, gu, gv, ga, gb, misc, pad], axis=-1).astype(BF16)


FOX_CH = 256


def _fox_prep_kernel(misc_ref, bf_ref, lf_ref, c_ref, carry):
    ch = misc_ref.shape[0]

    @pl.when(pl.program_id(1) == 0)
    def _():
        carry[...] = jnp.zeros_like(carry)

    lf = _log_sigmoid(misc_ref[...] + bf_ref[...])
    row = lax.broadcasted_iota(jnp.int32, (ch, ch), 0)
    col = lax.broadcasted_iota(jnp.int32, (ch, ch), 1)
    tri = jnp.where(row >= col, 1.0, 0.0).astype(BF16)
    cs = _dot01_left(tri, lf) + carry[...]
    carry[...] = cs[ch - 1:ch, :]
    lf_ref[...] = lf
    c_ref[...] = cs


def _fox_prep(h, bf_row, nb, seq):
    ch = FOX_CH
    nch = seq // ch
    t = nb * seq
    return pl.pallas_call(
        _fox_prep_kernel,
        grid=(nb, nch),
        in_specs=[pl.BlockSpec((ch, LANES), lambda b, c: (b * nch + c, COL_MISC // LANES)),
                  pl.BlockSpec((1, LANES), lambda b, c: (0, 0))],
        out_specs=[pl.BlockSpec((ch, LANES), lambda b, c: (b * nch + c, 0)),
                   pl.BlockSpec((ch, LANES), lambda b, c: (b * nch + c, 0))],
        out_shape=[jax.ShapeDtypeStruct((t, LANES), F32),
                   jax.ShapeDtypeStruct((t, LANES), F32)],
        scratch_shapes=[pltpu.VMEM((1, LANES), F32)],
        compiler_params=_cp(("arbitrary", "arbitrary")),
        name="fox_prep",
    )(h, bf_row)


def _fox_bias_lanes(c_col, q_side):
    hi, mid, lo = (t.astype(F32) for t in _split3(c_col))
    terms = (hi, mid, lo, 1.0, 1.0, 1.0) if q_side else (1.0, 1.0, 1.0, -hi, -mid, -lo)
    lane = lax.broadcasted_iota(jnp.int32, (1, FOX_HEAD_DIM), 1)
    ext = jnp.zeros((c_col.shape[0], FOX_HEAD_DIM), F32)
    for i, t in enumerate(terms):
        ext = jnp.where(lane == i, t, ext)
    return ext


def _fox_kernel(q_ref, k_ref, v_ref, c_ref, o_ref, kaug, vt, sc):
    tq = q_ref.shape[0]
    nblk = k_ref.shape[0] // tq
    qi = pl.program_id(1)
    hd = FOX_HEAD_DIM

    @pl.when(qi == 0)
    def _():
        for j in range(nblk):
            rows = slice(j * tq, (j + 1) * tq)
            kblk = k_ref[rows, :]
            cblk = c_ref[rows, :]
            for h in range(FOX_HEADS):
                kaug[h, rows, :] = jnp.concatenate(
                    [kblk[:, h * hd:(h + 1) * hd], _fox_bias_lanes(cblk[:, h:h + 1], False)], axis=1).astype(BF16)
            vt[j] = v_ref[rows, :].T.astype(BF16)

    qblk = q_ref[...] * (hd ** -0.5)
    cq = c_ref[pl.ds(pl.multiple_of(qi * tq, tq), tq), :]
    qaug = [jnp.concatenate([qblk[:, h * hd:(h + 1) * hd], _fox_bias_lanes(cq[:, h:h + 1], True)], axis=1).astype(BF16)
            for h in range(FOX_HEADS)]
    key = lax.broadcasted_iota(jnp.int32, (tq, tq), 0)
    qry = lax.broadcasted_iota(jnp.int32, (tq, tq), 1)
    causal = key <= qry
    nt = (((1,), (1,)), ((), ()))

    def score(j, stats, masked):
        start = pl.multiple_of(j * tq, tq)
        new = []
        for h in range(FOX_HEADS):
            m, l = stats[h]
            st = lax.dot_general(kaug[h, pl.ds(start, tq), :], qaug[h], nt, preferred_element_type=F32)
            if masked:
                st = jnp.where(causal, st, NEG)
            sc[h, j] = st
            m_new = jnp.maximum(m, jnp.max(st, axis=0, keepdims=True))
            l = jnp.exp(m - m_new) * l + jnp.sum(jnp.exp(st - m_new), axis=0, keepdims=True)
            new.append((m_new, l))
        return tuple(new)

    stats = tuple((jnp.full((1, tq), NEG, F32), jnp.zeros((1, tq), F32)) for _ in range(FOX_HEADS))
    stats = lax.fori_loop(0, qi, lambda j, c: score(j, c, False), stats)
    stats = score(qi, stats, True)

    def apply(j, accs):
        new = []
        for h in range(FOX_HEADS):
            m, l = stats[h]
            p = jnp.exp(sc[h, j] - m) / l
            new.append(accs[h] + jnp.dot(vt[j, h * hd:(h + 1) * hd, :], p.astype(BF16), preferred_element_type=F32))
        return tuple(new)

    accs = lax.fori_loop(0, qi + 1, apply, tuple(jnp.zeros((hd, tq), F32) for _ in range(FOX_HEADS)))
    o_ref[...] = jnp.concatenate(accs, axis=0).T


def _fox_prompt(h, c, nb, seq):
    tq = FOX_CH
    nq = seq // tq
    t = nb * seq
    return pl.pallas_call(
        _fox_kernel,
        grid=(nb, nq),
        in_specs=[pl.BlockSpec((tq, GROUP_W), lambda b, i: (b * nq + i, COL_Q // GROUP_W)),
                  pl.BlockSpec((seq, GROUP_W), lambda b, i: (b, COL_K // GROUP_W)),
                  pl.BlockSpec((seq, GROUP_W), lambda b, i: (b, COL_V // GROUP_W)),
                  pl.BlockSpec((seq, LANES), lambda b, i: (b, 0))],
        out_specs=pl.BlockSpec((tq, GROUP_W), lambda b, i: (b * nq + i, 0)),
        out_shape=jax.ShapeDtypeStruct((t, GROUP_W), F32),
        scratch_shapes=[pltpu.VMEM((FOX_HEADS, seq, 2 * FOX_HEAD_DIM), BF16),
                        pltpu.VMEM((nq, GROUP_W, tq), BF16),
                        pltpu.VMEM((FOX_HEADS, nq, tq, tq), F32)],
        compiler_params=_cp(("arbitrary", "arbitrary")),
        name="fox_prompt",
    )(h, h, h, c)


def _kv_state_kernel(k_ref, v_ref, *rest):
    ko_ref, vo_ref = rest[-2:]
    k = k_ref[...]
    v = v_ref[...]
    for h in range(FOX_HEADS):
        ko_ref[:, h, :] = k[:, h * FOX_HEAD_DIM:(h + 1) * FOX_HEAD_DIM]
        vo_ref[:, h, :] = v[:, h * FOX_HEAD_DIM:(h + 1) * FOX_HEAD_DIM]


def _kv_state(h, t, layer, depth, prev=None):
    tm = 512
    prev = () if prev is None else tuple(prev)
    osp = pl.BlockSpec((None, tm, FOX_HEADS, FOX_HEAD_DIM), lambda i: (layer, i, 0, 0))
    osd = jax.ShapeDtypeStruct((depth, t, FOX_HEADS, FOX_HEAD_DIM), F32)
    return pl.pallas_call(
        _kv_state_kernel,
        grid=(t // tm,),
        in_specs=[pl.BlockSpec((tm, GROUP_W), lambda i: (i, COL_K // GROUP_W)),
                  pl.BlockSpec((tm, GROUP_W), lambda i: (i, COL_V // GROUP_W))]
                 + [pl.BlockSpec(memory_space=pl.ANY)] * len(prev),
        out_specs=[osp, osp],
        out_shape=[osd, osd],
        input_output_aliases={2 + i: i for i in range(len(prev))},
        compiler_params=_cp(("arbitrary",)),
        name="kv_state",
    )(h, h, *prev)


DEC_PPS = 8


def _fox_dec_kernel(pt_ref, q_ref, kn_ref, vn_ref, lfn_ref, *rest):
    del pt_ref
    pps = DEC_PPS
    k_refs = rest[:pps]
    v_refs = rest[pps:2 * pps]
    lf_refs = rest[2 * pps:3 * pps]
    o_ref = rest[3 * pps]
    m_s, l_s, acc_s, carry_s, qm_s, snew_s, lg_s = rest[3 * pps + 1:]
    b = pl.program_id(0)
    s = pl.program_id(1)
    ns = pl.num_programs(1) // 2
    hrow = lax.broadcasted_iota(jnp.int32, (FOX_HEADS, GROUP_W), 0)
    hcol = lax.broadcasted_iota(jnp.int32, (FOX_HEADS, GROUP_W), 1) // FOX_HEAD_DIM
    headmask = hrow == hcol
    nt = (((1,), (1,)), ((), ()))

    @pl.when(s == 0)
    def _():
        qrow = q_ref[pl.ds(b, 1), :] * (FOX_HEAD_DIM ** -0.5)
        qm = jnp.where(headmask, jnp.broadcast_to(qrow, (FOX_HEADS, GROUP_W)), 0.0)
        qm_s[...] = qm.astype(BF16)
        kn = kn_ref[pl.ds(b, 1), :].astype(BF16).astype(F32)
        snew = jnp.sum(qm.astype(BF16).astype(F32) * kn, axis=-1, keepdims=True)
        snew_s[...] = snew
        m_s[...] = snew
        lfn = jnp.broadcast_to(lfn_ref[pl.ds(b, 1), :], (FOX_HEADS, LANES))
        er = lax.broadcasted_iota(jnp.int32, (FOX_HEADS, LANES), 0)
        ec = lax.broadcasted_iota(jnp.int32, (FOX_HEADS, LANES), 1)
        carry_s[...] = jnp.sum(jnp.where(er == ec, lfn, 0.0), axis=-1, keepdims=True)

    @pl.when(s < ns)
    def _():
        r = lax.broadcasted_iota(jnp.int32, (PAGE_SIZE, PAGE_SIZE), 0)
        c = lax.broadcasted_iota(jnp.int32, (PAGE_SIZE, PAGE_SIZE), 1)
        strict = jnp.where(r > c, 1.0, 0.0).astype(BF16)
        qm = qm_s[...]
        m, carry = m_s[...], carry_s[...]
        for i in range(pps):
            kp = k_refs[i][...].astype(BF16)
            lf = lf_refs[i][...]
            st = lax.dot_general(qm, kp, nt, preferred_element_type=F32)
            logits = st + (_dot01_right(lf, strict) + carry)
            carry = carry + jnp.sum(lf, axis=-1, keepdims=True)
            m = jnp.maximum(m, jnp.max(logits, axis=-1, keepdims=True))
            lg_s[s * pps + i] = logits
        m_s[...], carry_s[...] = m, carry

    @pl.when(s == ns - 1)
    def _():
        m = m_s[...]

        def add(j, l):
            return l + jnp.sum(jnp.exp(lg_s[j] - m), axis=-1, keepdims=True)

        l_s[...] = lax.fori_loop(0, ns * pps, add, jnp.exp(snew_s[...] - m))
        acc_s[...] = jnp.zeros_like(acc_s)

    @pl.when(s >= ns)
    def _():
        m, l = m_s[...], l_s[...]
        acc = acc_s[...]
        for i in range(pps):
            vp = v_refs[i][...].astype(BF16)
            p = jnp.exp(lg_s[(s - ns) * pps + i] - m) / l
            acc = acc + jnp.dot(p.astype(BF16), vp, preferred_element_type=F32)
        acc_s[...] = acc

    @pl.when(s == 2 * ns - 1)
    def _():
        m, l = m_s[...], l_s[...]
        pn = (jnp.exp(snew_s[...] - m) / l).astype(BF16).astype(F32)
        vn = jnp.broadcast_to(vn_ref[pl.ds(b, 1), :].astype(BF16).astype(F32), (FOX_HEADS, GROUP_W))
        o_ref[...] = jnp.sum(jnp.where(headmask, acc_s[...] + pn * vn, 0.0), axis=0, keepdims=True)


def _fox_decode(hs, rb, lfn, ck, cv, clft, page_table, layer):
    nb, npg = page_table.shape
    pps = DEC_PPS
    ns = npg // pps
    pt = page_table.reshape(-1)

    def page_map(i, phase):
        def index(b, s, pt_ref):
            sp = jnp.minimum(s, ns - 1) if phase == 0 else jnp.maximum(s - ns, 0)
            return (layer, pt_ref[b * npg + (npg - 1 - (sp * pps + i))], 0, 0)
        return index

    full = lambda cb: pl.BlockSpec((nb, GROUP_W), lambda b, s, pt_ref: (rb, cb))
    in_specs = [full(COL_Q // GROUP_W), full(COL_K // GROUP_W), full(COL_V // GROUP_W),
                pl.BlockSpec((nb, LANES), lambda b, s, pt_ref: (0, 0))]
    in_specs += [pl.BlockSpec((None, None, PAGE_SIZE, GROUP_W), page_map(i, 0)) for i in range(pps)]
    in_specs += [pl.BlockSpec((None, None, PAGE_SIZE, GROUP_W), page_map(i, 1)) for i in range(pps)]
    in_specs += [pl.BlockSpec((None, None, FOX_HEADS, PAGE_SIZE), page_map(i, 0)) for i in range(pps)]
    out = pl.pallas_call(
        _fox_dec_kernel,
        grid_spec=pltpu.PrefetchScalarGridSpec(
            num_scalar_prefetch=1,
            grid=(nb, 2 * ns),
            in_specs=in_specs,
            out_specs=pl.BlockSpec((None, 1, GROUP_W), lambda b, s, pt_ref: (b, 0, 0)),
            scratch_shapes=[pltpu.VMEM((FOX_HEADS, 1), F32), pltpu.VMEM((FOX_HEADS, 1), F32),
                            pltpu.VMEM((FOX_HEADS, GROUP_W), F32), pltpu.VMEM((FOX_HEADS, 1), F32),
                            pltpu.VMEM((FOX_HEADS, GROUP_W), BF16), pltpu.VMEM((FOX_HEADS, 1), F32),
                            pltpu.VMEM((npg, FOX_HEADS, PAGE_SIZE), F32)]),
        out_shape=jax.ShapeDtypeStruct((nb, 1, GROUP_W), F32),
        compiler_params=_cp(("arbitrary", "arbitrary")),
        name="fox_decode",
    )(pt, hs, hs, hs, lfn, *([ck] * pps), *([cv] * pps), *([clft] * pps))
    return out.reshape(nb, GROUP_W)


def _ssd_kernel(xbc_ref, z_ref, misc_ref, cw_ref, cb_ref, hp_ref, o_ref, st_ref, cs_ref, carry, hstate):
    q = SSD_CHUNK
    c = pl.program_id(1)
    nc = pl.num_programs(1)

    @pl.when(c == 0)
    def _():
        carry[...] = jnp.zeros_like(carry)
        hstate[...] = jnp.zeros_like(hstate)

    xin = xbc_ref[...]
    full = jnp.concatenate([carry[...], xin], axis=0).astype(BF16).astype(F32)
    cw = cw_ref[...].astype(BF16).astype(F32)
    conv = cb_ref[...] + sum(full[SUBLANES - (SSD_CONV - 1) + w:SUBLANES - (SSD_CONV - 1) + w + q, :] * cw[w:w + 1, :]
                             for w in range(SSD_CONV))
    carry[...] = xin[q - SUBLANES:q, :]
    cs_ref[...] = xin[q - SUBLANES:q, :]
    xc = _silu(conv)
    xs = xc[:, :GROUP_W]

    dt = _softplus(misc_ref[...] + hp_ref[0:1, :])
    a_row = -jnp.exp(hp_ref[1:2, :])
    d_row = hp_ref[2:3, :]
    da = dt * a_row
    row = lax.broadcasted_iota(jnp.int32, (q, q), 0)
    col = lax.broadcasted_iota(jnp.int32, (q, q), 1)
    tril = row >= col
    tril01 = jnp.where(tril, 1.0, 0.0).astype(BF16)
    triu01 = jnp.where(row <= col, 1.0, 0.0).astype(BF16)
    acs = _dot01_left(tril01, da)
    dt_t = dt.T
    acs_t = _dot01_right(da.T, triu01)
    nt = (((1,), (1,)), ((), ()))
    tn = (((0,), (0,)), ((), ()))
    rep = SSD_HEADS // SSD_GROUPS
    for g in range(SSD_GROUPS):
        bm = xc[:, GROUP_W + g * SSD_STATE:GROUP_W + (g + 1) * SSD_STATE].astype(BF16)
        cm = xc[:, GROUP_W + (SSD_GROUPS + g) * SSD_STATE:GROUP_W + (SSD_GROUPS + g + 1) * SSD_STATE].astype(BF16)
        cbm = lax.dot_general(cm, bm, nt, preferred_element_type=F32)
        for r in range(rep):
            hd = g * rep + r
            ln = MISC_DT + hd
            lo, hi = hd * SSD_HEAD_DIM, (hd + 1) * SSD_HEAD_DIM
            a_col = acs[:, ln:ln + 1]
            a_rw = acs_t[ln:ln + 1, :]
            dt_rw = dt_t[ln:ln + 1, :]
            dt_col = dt[:, ln:ln + 1]
            a_last = acs[q - 1:q, ln:ln + 1]
            decay = jnp.where(tril, jnp.exp(jnp.minimum(a_col - a_rw, 0.0)), 0.0)
            mm = (cbm * decay * dt_rw).astype(BF16)
            xh = xs[:, lo:hi]
            y = jnp.dot(mm, xh.astype(BF16), preferred_element_type=F32)
            xw = (xh * (jnp.exp(a_last - a_col) * dt_col)).astype(BF16)
            states = lax.dot_general(xw, bm, tn, preferred_element_type=F32)
            hprev = hstate[hd]
            y_off = lax.dot_general(cm, hprev.astype(BF16), nt, preferred_element_type=F32)
            y = y + y_off * jnp.exp(a_col) + d_row[:, ln:ln + 1] * xh
            hstate[hd] = hprev * jnp.exp(a_last) + states
            o_ref[:, lo:hi] = y
    o_ref[...] = o_ref[...] * _silu(z_ref[...])

    @pl.when(c == nc - 1)
    def _():
        st_ref[...] = hstate[...]


def _ssd_prompt(h, cw, cb, hp, nb, seq):
    q = SSD_CHUNK
    nc = seq // q
    t = nb * seq
    return pl.pallas_call(
        _ssd_kernel,
        grid=(nb, nc),
        in_specs=[pl.BlockSpec((q, SSD_CONV_DIM), lambda b, c: (b * nc + c, COL_XBC // SSD_CONV_DIM)),
                  pl.BlockSpec((q, GROUP_W), lambda b, c: (b * nc + c, COL_Z // GROUP_W)),
                  pl.BlockSpec((q, LANES), lambda b, c: (b * nc + c, COL_MISC // LANES)),
                  pl.BlockSpec((SSD_CONV, SSD_CONV_DIM), lambda b, c: (0, 0)),
                  pl.BlockSpec((1, SSD_CONV_DIM), lambda b, c: (0, 0)),
                  pl.BlockSpec((SUBLANES, LANES), lambda b, c: (0, 0))],
        out_specs=[pl.BlockSpec((q, GROUP_W), lambda b, c: (b * nc + c, 0)),
                   pl.BlockSpec((None, SSD_HEADS, SSD_HEAD_DIM, SSD_STATE), lambda b, c: (b, 0, 0, 0)),
                   pl.BlockSpec((None, SUBLANES, SSD_CONV_DIM), lambda b, c: (b, 0, 0))],
        out_shape=[jax.ShapeDtypeStruct((t, GROUP_W), F32),
                   jax.ShapeDtypeStruct((nb, SSD_HEADS, SSD_HEAD_DIM, SSD_STATE), F32),
                   jax.ShapeDtypeStruct((nb, SUBLANES, SSD_CONV_DIM), F32)],
        scratch_shapes=[pltpu.VMEM((SUBLANES, SSD_CONV_DIM), F32),
                        pltpu.VMEM((SSD_HEADS, SSD_HEAD_DIM, SSD_STATE), F32)],
        compiler_params=_cp(("arbitrary", "arbitrary")),
        name="ssd_prompt",
    )(h, h, h, cw, cb, hp)


def _gmlp_kernel(u_ref, v_ref, lng_ref, lnb_ref, ws_ref, bst_ref, o_ref):
    n = GMLP_CHUNK
    v = _layer_norm(_gelu(v_ref[...]), lng_ref[...], lnb_ref[...])
    row = lax.broadcasted_iota(jnp.int32, (n, n), 0)
    col = lax.broadcasted_iota(jnp.int32, (n, n), 1)
    tril = row >= col
    gc = GROUP_W // GMLP_GROUPS
    parts = []
    for g in range(GMLP_GROUPS):
        wm = jnp.where(tril, ws_ref[g], 0.0).astype(BF16)
        sg = jnp.dot(wm, v[:, g * gc:(g + 1) * gc].astype(BF16), preferred_element_type=F32)
        parts.append(sg + bst_ref[:, g:g + 1])
    s = jnp.concatenate(parts, axis=-1)
    o_ref[...] = _gelu(u_ref[...]) * s


def _gmlp_prompt(h, lng, lnb, ws, bst, t):
    n = GMLP_CHUNK
    return pl.pallas_call(
        _gmlp_kernel,
        grid=(t // n,),
        in_specs=[pl.BlockSpec((n, GROUP_W), lambda i: (i, COL_GU // GROUP_W)),
                  pl.BlockSpec((n, GROUP_W), lambda i: (i, COL_GV // GROUP_W)),
                  pl.BlockSpec((1, GROUP_W), lambda i: (0, 0)),
                  pl.BlockSpec((1, GROUP_W), lambda i: (0, 0)),
                  pl.BlockSpec((GMLP_GROUPS, n, n), lambda i: (0, 0, 0)),
                  pl.BlockSpec((n, LANES), lambda i: (0, 0))],
        out_specs=pl.BlockSpec((n, GROUP_W), lambda i: (i, 0)),
        out_shape=jax.ShapeDtypeStruct((t, GROUP_W), F32),
        compiler_params=_cp(("arbitrary",)),
        name="gmlp_prompt",
    )(h, h, lng, lnb, ws, bst)


CONF_HALO = 32
CONF_CHUNK = 128


def _conf_kernel(a_ref, g_ref, cw_ref, cb_ref, lng_ref, lnb_ref, o_ref, st_ref, carry):
    n = CONF_CHUNK

    @pl.when(pl.program_id(1) == 0)
    def _():
        carry[...] = jnp.zeros_like(carry)

    glu = a_ref[...] * _sigmoid(g_ref[...])
    full = jnp.concatenate([carry[...], glu], axis=0).astype(BF16).astype(F32)
    cw = cw_ref[...].astype(BF16).astype(F32)
    off = CONF_HALO - (CONF_KERNEL - 1)
    acc = cb_ref[...] + full[off:off + n, :] * cw[0:1, :]
    for w in range(1, CONF_KERNEL):
        acc = acc + full[off + w:off + w + n, :] * cw[w:w + 1, :]
    carry[...] = glu[n - CONF_HALO:n, :]
    st_ref[...] = glu[n - CONF_HALO:n, :]
    o_ref[...] = _silu(_layer_norm(acc, lng_ref[...], lnb_ref[...]))


def _conf_prompt(h, cw, cb, lng, lnb, nb, seq):
    n = CONF_CHUNK
    nc = seq // n
    t = nb * seq
    return pl.pallas_call(
        _conf_kernel,
        grid=(nb, nc),
        in_specs=[pl.BlockSpec((n, GROUP_W), lambda b, c: (b * nc + c, COL_GA // GROUP_W)),
                  pl.BlockSpec((n, GROUP_W), lambda b, c: (b * nc + c, COL_GB // GROUP_W)),
                  pl.BlockSpec((CONF_HALO, GROUP_W), lambda b, c: (0, 0)),
                  pl.BlockSpec((1, GROUP_W), lambda b, c: (0, 0)),
                  pl.BlockSpec((1, GROUP_W), lambda b, c: (0, 0)),
                  pl.BlockSpec((1, GROUP_W), lambda b, c: (0, 0))],
        out_specs=[pl.BlockSpec((n, GROUP_W), lambda b, c: (b * nc + c, 0)),
                   pl.BlockSpec((None, CONF_HALO, GROUP_W), lambda b, c: (b, 0, 0))],
        out_shape=[jax.ShapeDtypeStruct((t, GROUP_W), F32),
                   jax.ShapeDtypeStruct((nb, CONF_HALO, GROUP_W), F32)],
        scratch_shapes=[pltpu.VMEM((CONF_HALO, GROUP_W), F32)],
        compiler_params=_cp(("arbitrary", "arbitrary")),
        name="conf_prompt",
    )(h, h, cw, cb, lng, lnb)


def _dec_rows_kernel(gu_ref, gv_ref, ga_ref, gb_ref, xbc_ref, misc_ref,
                     lng_ref, lnb_ref, w00_ref, b0_ref,
                     ccw_ref, ccb_ref, clg_ref, clb_ref, cprev_ref,
                     scw_ref, scb_ref, sprev_ref, hp_ref,
                     c_ref, vrow_ref, d_ref, glu_ref, xc_ref, sm_ref):
    nb = gu_ref.shape[0]
    v = _layer_norm(_gelu(gv_ref[...]), lng_ref[...], lnb_ref[...])
    vrow_ref[...] = v
    c_ref[...] = _gelu(gu_ref[...]) * (w00_ref[...] * v + b0_ref[...])
    glu = ga_ref[...] * _sigmoid(gb_ref[...])
    glu_ref[...] = glu
    kw = CONF_KERNEL - 1
    rows = []
    for b in range(nb):
        rows.append(jnp.sum(cprev_ref[b] * ccw_ref[0:kw, :], axis=0, keepdims=True))
    conv = jnp.concatenate(rows, axis=0) + glu * ccw_ref[kw:kw + 1, :] + ccb_ref[...]
    d_ref[...] = _silu(_layer_norm(conv, clg_ref[...], clb_ref[...]))
    sk = SSD_CONV - 1
    rows = []
    for b in range(nb):
        rows.append(jnp.sum(sprev_ref[b] * scw_ref[0:sk, :], axis=0, keepdims=True))
    sconv = jnp.concatenate(rows, axis=0) + xbc_ref[...] * scw_ref[sk:sk + 1, :] + scb_ref[...]
    xc_ref[...] = _silu(sconv)
    misc = misc_ref[...]
    lane = lax.broadcasted_iota(jnp.int32, misc.shape, 1)
    lf = _log_sigmoid(misc + hp_ref[3:4, :])
    dt = _softplus(misc + hp_ref[0:1, :])
    sm_ref[...] = jnp.where(lane < MISC_DT, lf, dt)


def _dec_rows(hs, rb, nb, lng, lnb, w00, b0, ccw, ccb, clg, clb, cprev, scw, scb, sprev, hp):
    col = lambda c0, w: pl.BlockSpec((nb, w), lambda i: (rb, c0 // w))
    whole = lambda a: pl.BlockSpec(a.shape, lambda i: (0,) * a.ndim)
    sd = lambda w: jax.ShapeDtypeStruct((nb, w), F32)
    return pl.pallas_call(
        _dec_rows_kernel,
        grid=(1,),
        in_specs=[col(COL_GU, GROUP_W), col(COL_GV, GROUP_W), col(COL_GA, GROUP_W), col(COL_GB, GROUP_W),
                  col(COL_XBC, SSD_CONV_DIM), col(COL_MISC, LANES)]
                 + [whole(a) for a in (lng, lnb, w00, b0, ccw, ccb, clg, clb, cprev, scw, scb, sprev, hp)],
        out_specs=[pl.BlockSpec((nb, GROUP_W), lambda i: (0, 0))] * 4
                  + [pl.BlockSpec((nb, SSD_CONV_DIM), lambda i: (0, 0)), pl.BlockSpec((nb, LANES), lambda i: (0, 0))],
        out_shape=[sd(GROUP_W)] * 4 + [sd(SSD_CONV_DIM), sd(LANES)],
        compiler_params=_cp(("arbitrary",)),
        name="dec_rows",
    )(hs, hs, hs, hs, hs, hs, lng, lnb, w00, b0, ccw, ccb, clg, clb, cprev, scw, scb, sprev, hp)


def _dec_ssm_kernel(xt_ref, zt_ref, bc_ref, sm_ref, hp_ref, h_ref, yt_ref, hn_ref):
    nb = bc_ref.shape[0]
    rep = SSD_HEADS // SSD_GROUPS
    a_row = -jnp.exp(hp_ref[1:2, :])
    d_row = hp_ref[2:3, :]
    sm = sm_ref[...]
    for b in range(nb):
        cols = []
        for hd in range(SSD_HEADS):
            g = hd // rep
            ln = MISC_DT + hd
            lo, hi = hd * SSD_HEAD_DIM, (hd + 1) * SSD_HEAD_DIM
            dt = sm[b:b + 1, ln:ln + 1]
            x = xt_ref[lo:hi, b:b + 1]
            brow = bc_ref[b:b + 1, g * SSD_STATE:(g + 1) * SSD_STATE]
            crow = bc_ref[b:b + 1, (SSD_GROUPS + g) * SSD_STATE:(SSD_GROUPS + g + 1) * SSD_STATE]
            hnew = h_ref[b, hd] * jnp.exp(dt * a_row[:, ln:ln + 1]) + (dt * x) * brow
            hn_ref[b, hd] = hnew
            y = jnp.sum(hnew.astype(BF16).astype(F32) * crow.astype(BF16).astype(F32), axis=-1, keepdims=True)
            cols.append(y + d_row[:, ln:ln + 1] * x)
        ycol = jnp.concatenate(cols, axis=0)
        yt_ref[:, b:b + 1] = ycol * _silu(zt_ref[:, b:b + 1])


def _dec_ssm(xt, zt, bc, sm, hp, hstate):
    nb = bc.shape[0]
    whole = lambda a: pl.BlockSpec(a.shape, lambda i: (0,) * a.ndim)
    return pl.pallas_call(
        _dec_ssm_kernel,
        grid=(1,),
        in_specs=[whole(a) for a in (xt, zt, bc, sm, hp, hstate)],
        out_specs=[pl.BlockSpec((GROUP_W, nb), lambda i: (0, 0)),
                   pl.BlockSpec(hstate.shape, lambda i: (0, 0, 0, 0))],
        out_shape=[jax.ShapeDtypeStruct((GROUP_W, nb), F32), jax.ShapeDtypeStruct(hstate.shape, F32)],
        compiler_params=_cp(("arbitrary",)),
        name="dec_ssm",
    )(xt, zt, bc, sm, hp, hstate)


def _pack_bf16_pair(lo, hi):
    lo_bits = lax.bitcast_convert_type(lo.astype(BF16).astype(F32), jnp.uint32)
    hi_bits = lax.bitcast_convert_type(hi.astype(BF16).astype(F32), jnp.uint32)
    return (lo_bits >> 16) | (hi_bits & jnp.uint32(0xFFFF0000))


def _unpack_bf16_pair(w):
    lo = lax.bitcast_convert_type(w << 16, F32).astype(BF16)
    hi = lax.bitcast_convert_type(w & jnp.uint32(0xFFFF0000), F32).astype(BF16)
    return lo, hi


def _outproj_kernel(a_ref, b_ref, c_ref, d_ref, x_ref, w_ref, mg_ref, g_ref, bb_ref, rw_ref, rb_ref, *rest):
    h_ref, hb_ref, ti_ref, tg_ref = rest[-4:]
    y = None
    for i, m_ref in enumerate((a_ref, b_ref, c_ref, d_ref)):
        m = m_ref[...]
        r = lax.rsqrt(jnp.mean(m * m, axis=-1, keepdims=True) + EPS)
        nrm = (m * r * mg_ref[:, i * GROUP_W:(i + 1) * GROUP_W]).astype(BF16)
        part = jnp.dot(nrm, w_ref[i * GROUP_W:(i + 1) * GROUP_W, :], preferred_element_type=F32)
        y = part if y is None else y + part
    h = _layer_norm(DEEPNORM_ALPHA * x_ref[...] + y, g_ref[...], bb_ref[...])
    h_ref[...] = h
    hb_ref[...] = _pack_bf16_pair(h[:, :D_MODEL // 2], h[:, D_MODEL // 2:])
    lg = jnp.dot(h.astype(BF16), rw_ref[...].astype(BF16), preferred_element_type=F32) + rb_ref[...]
    lane = lax.broadcasted_iota(jnp.int32, lg.shape, 1)
    ti = jnp.zeros(lg.shape, jnp.int32)
    tv = jnp.full(lg.shape, NEG, F32)
    for k in range(TOP_K):
        mx = jnp.max(lg, axis=-1, keepdims=True)
        idx = jnp.min(jnp.where(lg == mx, lane, LANES), axis=-1, keepdims=True)
        ti = jnp.where(lane == k, idx, ti)
        tv = jnp.where(lane == k, mx, tv)
        lg = jnp.where(lane == idx, NEG, lg)
    e = jnp.exp(tv - jnp.max(tv, axis=-1, keepdims=True))
    tg_ref[...] = e / jnp.sum(e, axis=-1, keepdims=True)
    ti_ref[...] = ti


def _outproj(a, b, c, d, x_all, w, mg, g, bb, rw, rb, tm, row0, prev=None):
    t_all = x_all.shape[0]
    n = a.shape[0]
    off = row0 // tm
    row = lambda w_: pl.BlockSpec((tm, w_), lambda i: (i, 0))
    row_off = lambda w_: pl.BlockSpec((tm, w_), lambda i: (i + off, 0))
    whole = lambda arr: pl.BlockSpec(arr.shape, lambda i: (0,) * arr.ndim)
    prev = () if prev is None else tuple(prev)
    n_in = 11
    return pl.pallas_call(
        _outproj_kernel,
        grid=(n // tm,),
        in_specs=[row(GROUP_W)] * 4 + [row_off(D_MODEL)] + [whole(arr) for arr in (w, mg, g, bb, rw, rb)]
                 + [pl.BlockSpec(memory_space=pl.ANY)] * len(prev),
        out_specs=[row_off(D_MODEL), row_off(D_MODEL // 2), row_off(LANES), row_off(LANES)],
        out_shape=[jax.ShapeDtypeStruct((t_all, D_MODEL), F32), jax.ShapeDtypeStruct((t_all, D_MODEL // 2), jnp.uint32),
                   jax.ShapeDtypeStruct((t_all, LANES), jnp.int32), jax.ShapeDtypeStruct((t_all, LANES), F32)],
        input_output_aliases={n_in + i: i for i in range(len(prev))},
        compiler_params=_cp(("arbitrary",)),
        name="outproj_ln_router",
    )(a, b, c, d, x_all, w, mg, g, bb, rw, rb, *prev)


def _moe_kernel(ge_ref, gb_ref, gr_ref, x_ref, wg_ref, wu_ref, wd_ref, bg_ref, bu_ref, bd_ref, y_ref,
                xb, wgu, wdb):
    del ge_ref, gb_ref
    g = pl.program_id(0)
    f = pl.program_id(1)
    rows = gr_ref[g]
    half = D_MODEL // 2

    @pl.when(rows > 0)
    def _():
        @pl.when(f == 0)
        def _():
            lo, hi = _unpack_bf16_pair(x_ref[...])
            xb[:, :half] = lo
            xb[:, half:] = hi
            y_ref[...] = jnp.broadcast_to(bd_ref[...], y_ref.shape)

        wgu[:, :MOE_TF] = wg_ref[...].astype(BF16)
        wgu[:, MOE_TF:] = wu_ref[...].astype(BF16)
        wdb[...] = wd_ref[...].astype(BF16)
        hgu = jnp.dot(xb[...], wgu[...], preferred_element_type=F32)
        gate = jnp.minimum(hgu[:, :MOE_TF] + bg_ref[...], SWIGLU_LIMIT)
        up = jnp.clip(hgu[:, MOE_TF:] + bu_ref[...], -SWIGLU_LIMIT, SWIGLU_LIMIT)
        act = (up + 1.0) * (gate * _sigmoid(SWIGLU_ALPHA * gate))
        y_ref[...] += jnp.dot(act.astype(BF16), wdb[...], preferred_element_type=F32)


def _moe_ffn(x_rows, w_gu, b_gu, w_dn, b_dn, ge, gb, gr):
    ng = ge.shape[0]
    d = D_MODEL
    nf = D_FF // MOE_TF
    r = MOE_R

    def fidx(g, f, gr_ref):
        return jnp.where(gr_ref[g] > 0, f, nf - 1)

    in_specs = [
        pl.BlockSpec((r, d // 2), lambda g, f, ge_, gb_, gr_: (gb_[g], 0)),
        pl.BlockSpec((None, d, MOE_TF), lambda g, f, ge_, gb_, gr_: (ge_[g], 0, fidx(g, f, gr_))),
        pl.BlockSpec((None, d, MOE_TF), lambda g, f, ge_, gb_, gr_: (ge_[g], 0, nf + fidx(g, f, gr_))),
        pl.BlockSpec((None, MOE_TF, d), lambda g, f, ge_, gb_, gr_: (ge_[g], fidx(g, f, gr_), 0)),
        pl.BlockSpec((None, 1, MOE_TF), lambda g, f, ge_, gb_, gr_: (ge_[g], 0, fidx(g, f, gr_))),
        pl.BlockSpec((None, 1, MOE_TF), lambda g, f, ge_, gb_, gr_: (ge_[g], 0, nf + fidx(g, f, gr_))),
        pl.BlockSpec((None, 1, d), lambda g, f, ge_, gb_, gr_: (ge_[g], 0, 0)),
    ]
    return pl.pallas_call(
        _moe_kernel,
        grid_spec=pltpu.PrefetchScalarGridSpec(
            num_scalar_prefetch=3,
            grid=(ng, nf),
            in_specs=in_specs,
            out_specs=pl.BlockSpec((r, d), lambda g, f, ge_, gb_, gr_: (gb_[g], 0)),
            scratch_shapes=[pltpu.VMEM((r, d), BF16), pltpu.VMEM((d, 2 * MOE_TF), BF16),
                            pltpu.VMEM((MOE_TF, d), BF16)]),
        out_shape=jax.ShapeDtypeStruct((ng * r, d), F32),
        compiler_params=_cp(("arbitrary", "arbitrary")),
        name="moe_ffn",
    )(ge, gb, gr, x_rows, w_gu, w_gu, w_dn, b_gu.reshape(N_EXPERTS, 1, -1), b_gu.reshape(N_EXPERTS, 1, -1),
      b_dn.reshape(N_EXPERTS, 1, -1))


def _moe_plan(top_i, n_tok):
    n_pairs = n_tok * TOP_K
    r = MOE_R
    ng = n_pairs // r + N_EXPERTS
    flat_e = top_i.reshape(-1)
    onehot = (flat_e[:, None] == jnp.arange(N_EXPERTS, dtype=jnp.int32)[None, :]).astype(jnp.int32)
    csum = jnp.cumsum(onehot, axis=0)
    counts = csum[-1]
    rank = jnp.sum(onehot * csum, axis=1) - 1
    ngrp = (counts + r - 1) // r
    grp_end = jnp.cumsum(ngrp)
    grp_start = grp_end - ngrp
    start = jnp.cumsum(counts) - counts
    dest = jnp.sum(onehot * grp_start[None, :], axis=1) * r + rank
    sorted_tok = (jnp.argsort(flat_e, stable=True) // TOP_K).astype(jnp.int32)
    total = grp_end[-1]
    gidx = jnp.arange(ng, dtype=jnp.int32)
    valid = gidx < total
    gclamp = jnp.minimum(gidx, total - 1)
    ge = jnp.sum(grp_end[None, :] <= gclamp[:, None], axis=1).astype(jnp.int32)
    local = gclamp - grp_start[ge]
    gr = jnp.where(valid, jnp.clip(counts[ge] - local * r, 0, r), 0).astype(jnp.int32)
    within = local[:, None] * r + jnp.arange(r, dtype=jnp.int32)[None, :]
    src = jnp.clip(start[ge][:, None] + within, 0, n_pairs - 1)
    row_tok = jnp.where(valid[:, None] & (within < counts[ge][:, None]), sorted_tok[src], 0)
    return row_tok.astype(jnp.int32), dest.reshape(n_tok, TOP_K).astype(jnp.int32), ge, gclamp.astype(jnp.int32), gr, total


def _gather_kernel(cnt_ref, idx_ref, src, out, idx_s, sem_i, sem_g):
    g = pl.program_id(0)
    n = cnt_ref[0]
    nr, nc = idx_s.shape
    ch = nr * nc

    @pl.when(g < n)
    def _():
        cp = pltpu.make_async_copy(idx_ref, idx_s, sem_i)
        cp.start()
        cp.wait()
        base = g * ch
        for r in range(nr):
            def body(c, carry, r=r):
                pltpu.make_async_copy(src.at[pl.ds(idx_s[r, c], 1)], out.at[pl.ds(base + r * nc + c, 1)], sem_g).start()
                return carry
            lax.fori_loop(0, nc, body, 0, unroll=8)
        chunk_done = pltpu.make_async_copy(out.at[pl.ds(0, ch)], out.at[pl.ds(0, ch)], sem_g)

        @pl.when(g > 0)
        def _():
            chunk_done.wait()

        @pl.when(g == n - 1)
        def _():
            chunk_done.wait()


def _gather_rows(src, idx, cnt, ch):
    p = idx.shape[0]
    steps = p // ch
    nr = SUBLANES
    nc = ch // nr
    return pl.pallas_call(
        _gather_kernel,
        grid_spec=pltpu.PrefetchScalarGridSpec(
            num_scalar_prefetch=1,
            grid=(steps,),
            in_specs=[pl.BlockSpec((nr, nc), lambda g, c: (g, 0)), pl.BlockSpec(memory_space=pl.ANY)],
            out_specs=pl.BlockSpec(memory_space=pl.ANY),
            scratch_shapes=[pltpu.SMEM((nr, nc), jnp.int32), pltpu.SemaphoreType.DMA, pltpu.SemaphoreType.DMA]),
        out_shape=jax.ShapeDtypeStruct((p, src.shape[1]), src.dtype),
        compiler_params=_cp(("arbitrary",)),
        name="row_gather",
    )(cnt, idx.reshape(steps * nr, nc), src)


def _combine_kernel(y0_ref, y1_ref, y2_ref, y3_ref, tg_ref, h_ref, g_ref, b_ref, o_ref):
    tg = tg_ref[...]
    ffn = tg[:, 0:1] * y0_ref[...]
    for k, y_ref in enumerate((y1_ref, y2_ref, y3_ref), start=1):
        ffn = ffn + tg[:, k:k + 1] * y_ref[...]
    o_ref[...] = _layer_norm(DEEPNORM_ALPHA * h_ref[...] + ffn, g_ref[...], b_ref[...])


def _combine(yg, tg, h, g, b, tm):
    t, d = h.shape
    ysp = lambda k: pl.BlockSpec((None, tm, d), lambda i: (k, i, 0))
    return pl.pallas_call(
        _combine_kernel,
        grid=(pl.cdiv(t, tm),),
        in_specs=[ysp(0), ysp(1), ysp(2), ysp(3),
                  pl.BlockSpec((tm, LANES), lambda i: (i, 0)),
                  pl.BlockSpec((tm, d), lambda i: (i, 0)),
                  pl.BlockSpec((1, d), lambda i: (0, 0)),
                  pl.BlockSpec((1, d), lambda i: (0, 0))],
        out_specs=pl.BlockSpec((tm, d), lambda i: (i, 0)),
        out_shape=jax.ShapeDtypeStruct((t, d), F32),
        compiler_params=_cp(("arbitrary",)),
        name="combine_ln2",
    )(yg, yg, yg, yg, tg, h, g, b)


def _lane_row(vals, offset, width=LANES):
    row = jnp.zeros((width,), F32)
    return lax.dynamic_update_slice(row, vals.astype(F32), (offset,)).reshape(1, width)


def _layer(x_all, P, past, nb, seq, nd, kv_prev):
    tp = nb * seq
    n_tok = tp + nd
    rb = tp // nd
    w_in = _prep_w_in(P['w_in'])
    hp_ = _matmul(x_all, w_in, 512, IN_TN)
    hs_ = hp_[tp:]

    hp = jnp.concatenate([_lane_row(P['ssd_dt_bias'], MISC_DT), _lane_row(P['ssd_a_log'], MISC_DT),
                          _lane_row(P['ssd_d'], MISC_DT), _lane_row(P['b_forget'], MISC_F),
                          jnp.zeros((4, LANES), F32)], axis=0)
    bf_row = hp[3:4]
    r1 = lambda a: a.reshape(1, -1).astype(F32)

    logf, c = _fox_prep(hp_, bf_row, nb, seq)
    a_out = _fox_prompt(hp_, c, nb, seq)
    b_out, ssm_p, sconv_p = _ssd_prompt(hp_, P['ssd_conv_w'], r1(P['ssd_conv_b']), hp, nb, seq)
    bst = jnp.pad(P['gmlp_bs'].T, ((0, 0), (0, LANES - GMLP_GROUPS)))
    c_out = _gmlp_prompt(hp_, r1(P['gmlp_ln_g']), r1(P['gmlp_ln_b']), P['gmlp_ws'], bst, tp)
    ccw = jnp.pad(P['conf_conv_w'], ((0, CONF_HALO - CONF_KERNEL), (0, 0)))
    d_out, conf_p = _conf_prompt(hp_, ccw, r1(P['conf_conv_b']), r1(P['conf_ln_g']), r1(P['conf_ln_b']), nb, seq)

    kv = _kv_state(hp_, tp, past['layer'], past['depth'], kv_prev)
    logf_p = logf[:, :FOX_HEADS].reshape(nb, seq, FOX_HEADS)
    state_p = (logf_p, ssm_p, sconv_p[:, SUBLANES - (SSD_CONV - 1):, :],
               conf_p[:, CONF_HALO - (CONF_KERNEL - 1):, :])

    w00 = jnp.repeat(P['gmlp_ws'][:, 0, 0], GROUP_W // GMLP_GROUPS).reshape(1, GROUP_W)
    b0 = jnp.repeat(P['gmlp_bs'][:, 0], GROUP_W // GMLP_GROUPS).reshape(1, GROUP_W)
    scw = jnp.pad(P['ssd_conv_w'], ((0, SUBLANES - SSD_CONV), (0, 0)))
    c_s, vrow_s, d_s, glu_s, xc_s, sm_s = _dec_rows(
        hp_, rb, nd, r1(P['gmlp_ln_g']), r1(P['gmlp_ln_b']), w00, b0,
        ccw, r1(P['conf_conv_b']), r1(P['conf_ln_g']), r1(P['conf_ln_b']), past['state_conv'],
        scw, r1(P['ssd_conv_b']), past['state_ssd_conv'], hp)
    yt, ssm_s = _dec_ssm(xc_s[:, :GROUP_W].T, hs_[:, COL_Z:COL_Z + GROUP_W].T, xc_s[:, GROUP_W:], sm_s, hp,
                         past['state_ssm'])
    b_s = yt.T
    a_s = _fox_decode(hp_, rb, sm_s, past['cache_k'], past['cache_v'], past['cache_lft'], past['page_table'],
                      past['layer'])
    k_s = hs_[:, COL_K:COL_K + GROUP_W].reshape(nd, 1, FOX_HEADS, FOX_HEAD_DIM)
    v_s = hs_[:, COL_V:COL_V + GROUP_W].reshape(nd, 1, FOX_HEADS, FOX_HEAD_DIM)
    logf_s = sm_s[:, :FOX_HEADS].reshape(nd, 1, FOX_HEADS)
    sconv_s = jnp.concatenate([past['state_ssd_conv'][:, 1:], hs_[:, None, COL_XBC:COL_XBC + SSD_CONV_DIM]], axis=1)
    conf_s = jnp.concatenate([past['state_conv'][:, 1:], glu_s[:, None, :]], axis=1)
    state_s = (k_s, v_s, logf_s, ssm_s, sconv_s, conf_s, vrow_s.reshape(nd, 1, GROUP_W))

    w_out = P['w_out'].astype(BF16)
    rw = jnp.pad(P['router_w'], ((0, 0), (0, LANES - N_EXPERTS)))
    rb = jnp.concatenate([P['router_b'].astype(F32), jnp.full((LANES - N_EXPERTS,), NEG, F32)]).reshape(1, LANES)
    op_args = (w_out, r1(P['mix_norm_g']), r1(P['ln1_g']), r1(P['ln1_b']), rw, rb)
    outs = _outproj(a_out, b_out, c_out, d_out, x_all, *op_args, tm=256, row0=0)
    h_all, hpk_all, ti_all, tg_all = _outproj(a_s, b_s, c_s, d_s, x_all, *op_args, tm=nd, row0=tp, prev=outs)

    row_tok, dest, ge, gb, gr, total = _moe_plan(ti_all[:, :TOP_K], n_tok)
    x_rows = _gather_rows(hpk_all, row_tok.reshape(-1), total.reshape(1), MOE_R)
    y_rows = _moe_ffn(x_rows, P['exp_w_gu'], P['exp_b_gu'], P['exp_w_dn'], P['exp_b_dn'], ge, gb, gr)
    n_pairs = n_tok * TOP_K
    steps = max(s for s in range(1, 33) if (n_pairs // SUBLANES) % s == 0)
    yg = _gather_rows(y_rows, dest.T.reshape(-1), jnp.full((1,), steps, jnp.int32), n_pairs // steps)
    o_all = _combine(yg.reshape(TOP_K, n_tok, D_MODEL), tg_all, h_all, r1(P['ln2_g']), r1(P['ln2_b']), tm=256)
    return o_all, state_p, state_s, kv


PARAM_NAMES = ('w_in', 'b_forget', 'ssd_conv_w', 'ssd_conv_b', 'ssd_dt_bias', 'ssd_a_log', 'ssd_d', 'gmlp_ln_g',
               'gmlp_ln_b', 'gmlp_ws', 'gmlp_bs', 'conf_conv_w', 'conf_conv_b', 'conf_ln_g', 'conf_ln_b',
               'mix_norm_g', 'w_out', 'ln1_g', 'ln1_b', 'router_w', 'router_b', 'exp_w_gu', 'exp_b_gu',
               'exp_w_dn', 'exp_b_dn', 'ln2_g', 'ln2_b')


def kernel(x_prompt, x_sample, cache_k, cache_v, cache_logf, state_ssm, state_ssd_conv, state_conv, page_table, w_in, b_forget, ssd_conv_w, ssd_conv_b, ssd_dt_bias, ssd_a_log, ssd_d, gmlp_ln_g, gmlp_ln_b, gmlp_ws, gmlp_bs, conf_conv_w, conf_conv_b, conf_ln_g, conf_ln_b, mix_norm_g, w_out, ln1_g, ln1_b, router_w, router_b, exp_w_gu, exp_b_gu, exp_w_dn, exp_b_dn, ln2_g, ln2_b):
    weights = (w_in, b_forget, ssd_conv_w, ssd_conv_b, ssd_dt_bias, ssd_a_log, ssd_d, gmlp_ln_g, gmlp_ln_b, gmlp_ws,
               gmlp_bs, conf_conv_w, conf_conv_b, conf_ln_g, conf_ln_b, mix_norm_g, w_out, ln1_g, ln1_b, router_w,
               router_b, exp_w_gu, exp_b_gu, exp_w_dn, exp_b_dn, ln2_g, ln2_b)
    nb, seq, d = x_prompt.shape
    nd = x_sample.shape[0]
    depth = w_in.shape[0]
    n_pool = cache_k.shape[1]
    ck = cache_k.reshape(depth, n_pool, PAGE_SIZE, GROUP_W)
    cv = cache_v.reshape(depth, n_pool, PAGE_SIZE, GROUP_W)
    clft = jnp.swapaxes(cache_logf, 2, 3)
    tp = nb * seq
    x_all = jnp.concatenate([x_prompt.reshape(tp, d), x_sample.reshape(nd, d)], axis=0)
    st_p, st_s = [], []
    kv = None
    for l in range(depth):
        P = {n: w[l] for n, w in zip(PARAM_NAMES, weights)}
        past = dict(cache_k=ck, cache_v=cv, cache_lft=clft, layer=l, depth=depth, state_ssm=state_ssm[l],
                    state_ssd_conv=state_ssd_conv[l], state_conv=state_conv[l], page_table=page_table)
        x_all, sp, ss, kv = _layer(x_all, P, past, nb, seq, nd, kv)
        st_p.append(sp)
        st_s.append(ss)
    stack = lambda sts, i: jnp.stack([s[i] for s in sts], axis=0)
    kv_shape = (depth, nb, seq, FOX_HEADS, FOX_HEAD_DIM)
    return (x_all[:tp].reshape(nb, seq, d), x_all[tp:].reshape(nd, 1, d),
            kv[0].reshape(kv_shape), kv[1].reshape(kv_shape), stack(st_p, 0),
            stack(st_s, 0), stack(st_s, 1), stack(st_s, 2),
            stack(st_p, 1), stack(st_s, 3), stack(st_p, 2), stack(st_s, 4),
            stack(st_p, 3), stack(st_s, 5), stack(st_s, 6))
```

```python
import functools
import math

import jax
import jax.numpy as jnp
from jax import lax
from jax.experimental import pallas as pl
from jax.experimental.pallas import tpu as pltpu

F32 = jnp.float32
BF16 = jnp.bfloat16

D_MODEL = 2048
GROUP_W = 512
N_MIXERS = 4
FOX_HEADS = 8
FOX_HEAD_DIM = 64
SSD_HEADS = 8
SSD_HEAD_DIM = 64
SSD_STATE = 128
SSD_GROUPS = 2
SSD_CONV = 4
SSD_CHUNK = 128
SSD_CONV_DIM = GROUP_W + 2 * SSD_GROUPS * SSD_STATE
GMLP_CHUNK = 128
GMLP_GROUPS = 4
CONF_KERNEL = 31
IN_SIZES = (GROUP_W, GROUP_W, GROUP_W, FOX_HEADS, GROUP_W, SSD_CONV_DIM, SSD_HEADS,
            GROUP_W, GROUP_W, GROUP_W, GROUP_W)
IN_SPLITS = tuple(sum(IN_SIZES[:i + 1]) for i in range(len(IN_SIZES) - 1))
N_EXPERTS = 32
TOP_K = 4
D_FF = D_MODEL
SWIGLU_LIMIT = 7.0
SWIGLU_ALPHA = 1.702
DEPTH = 2
DEEPNORM_ALPHA = (2 * DEPTH) ** 0.25
EPS = 1e-5
PAGE_SIZE = 128

LANES = 128
SUBLANES = 8
VMEM_LIMIT = 56 * 1024 * 1024

COL_Q, COL_K, COL_V, COL_Z, COL_XBC, COL_GU, COL_GV, COL_GA, COL_GB, COL_MISC = (
    0, 512, 1024, 1536, 2048, 3072, 3584, 4096, 4608, 5120)
MISC_F = 0
MISC_DT = 8
IN_COLS_PAD = 5376
IN_TN = 1792
NEG = -1e30

MOE_R = 1152
MOE_TF = 256
ROW_TILES_X = D_MODEL // 2 // LANES
ROW_TILES_Y = D_MODEL // LANES


def _cp(sem, vmem=VMEM_LIMIT):
    return pltpu.CompilerParams(dimension_semantics=sem, vmem_limit_bytes=vmem)


def _split3(x):
    hi = x.astype(BF16)
    r1 = x - hi.astype(F32)
    mid = r1.astype(BF16)
    lo = (r1 - mid.astype(F32)).astype(BF16)
    return hi, mid, lo


def _dot01_left(t01, x):
    hi, mid, lo = _split3(x)
    d = lambda b: jnp.dot(t01, b, preferred_element_type=F32)
    return d(hi) + d(mid) + d(lo)


def _dot01_right(x, t01):
    hi, mid, lo = _split3(x)
    d = lambda a: jnp.dot(a, t01, preferred_element_type=F32)
    return d(hi) + d(mid) + d(lo)


def _log_sigmoid(x):
    return jnp.minimum(x, 0.0) - jnp.log(1.0 + jnp.exp(-jnp.abs(x)))


def _softplus(x):
    return jnp.maximum(x, 0.0) + jnp.log(1.0 + jnp.exp(-jnp.abs(x)))


def _sigmoid(x):
    return 1.0 / (1.0 + jnp.exp(-x))


def _silu(x):
    return x * _sigmoid(x)


def _gelu(x):
    return 0.5 * x * (1.0 + jnp.tanh(math.sqrt(2.0 / math.pi) * (x + 0.044715 * (x * x * x))))


def _layer_norm(x, g, b):
    mu = jnp.mean(x, axis=-1, keepdims=True)
    xc = x - mu
    var = jnp.mean(xc * xc, axis=-1, keepdims=True)
    return xc * lax.rsqrt(var + EPS) * g + b


def _mm_kernel(x_ref, w_ref, o_ref):
    o_ref[...] = jnp.dot(x_ref[...].astype(BF16), w_ref[...], preferred_element_type=F32)


def _matmul(x, w, tm, tn):
    m, k = x.shape
    n = w.shape[1]
    return pl.pallas_call(
        _mm_kernel,
        grid=(n // tn, pl.cdiv(m, tm)),
        in_specs=[pl.BlockSpec((tm, k), lambda j, i: (i, 0)),
                  pl.BlockSpec((k, tn), lambda j, i: (0, j))],
        out_specs=pl.BlockSpec((tm, tn), lambda j, i: (i, j)),
        out_shape=jax.ShapeDtypeStruct((m, n), F32),
        compiler_params=_cp(("arbitrary", "arbitrary")),
        name="in_proj",
    )(x, w)


def _prep_w_in(w):
    q, k, v, f, z, xbc, dt, gu, gv, ga, gb = jnp.split(w, IN_SPLITS, axis=-1)
    d = w.shape[0]
    misc = jnp.concatenate([f, dt, jnp.zeros((d, LANES - 16), w.dtype)], axis=-1)
    pad = jnp.zeros((d, IN_COLS_PAD - (COL_MISC + LANES)), w.dtype)
    return jnp.concatenate([q, k, v, z, xbc, gu, gv, ga, gb, misc, pad], axis=-1).astype(BF16)


FOX_CH = 256


def _fox_prep_kernel(misc_ref, bf_ref, lf_ref, c_ref, carry):
    ch = misc_ref.shape[0]

    @pl.when(pl.program_id(1) == 0)
    def _():
        carry[...] = jnp.zeros_like(carry)

    lf = _log_sigmoid(misc_ref[...] + bf_ref[...])
    row = lax.broadcasted_iota(jnp.int32, (ch, ch), 0)
    col = lax.broadcasted_iota(jnp.int32, (ch, ch), 1)
    tri = jnp.where(row >= col, 1.0, 0.0).astype(BF16)
    cs = _dot01_left(tri, lf) + carry[...]
    carry[...] = cs[ch - 1:ch, :]
    lf_ref[...] = lf
    c_ref[...] = cs


def _fox_prep(h, bf_row, nb, seq):
    ch = FOX_CH
    nch = seq // ch
    t = nb * seq
    return pl.pallas_call(
        _fox_prep_kernel,
        grid=(nb, nch),
        in_specs=[pl.BlockSpec((ch, LANES), lambda b, c: (b * nch + c, COL_MISC // LANES)),
                  pl.BlockSpec((1, LANES), lambda b, c: (0, 0))],
        out_specs=[pl.BlockSpec((ch, LANES), lambda b, c: (b * nch + c, 0)),
                   pl.BlockSpec((ch, LANES), lambda b, c: (b * nch + c, 0))],
        out_shape=[jax.ShapeDtypeStruct((t, LANES), F32),
                   jax.ShapeDtypeStruct((t, LANES), F32)],
        scratch_shapes=[pltpu.VMEM((1, LANES), F32)],
        compiler_params=_cp(("arbitrary", "arbitrary")),
        name="fox_prep",
    )(h, bf_row)


def _fox_bias_lanes(c_col, q_side):
    hi, mid, lo = (t.astype(F32) for t in _split3(c_col))
    terms = (hi, mid, lo, 1.0, 1.0, 1.0) if q_side else (1.0, 1.0, 1.0, -hi, -mid, -lo)
    lane = lax.broadcasted_iota(jnp.int32, (1, FOX_HEAD_DIM), 1)
    ext = jnp.zeros((c_col.shape[0], FOX_HEAD_DIM), F32)
    for i, t in enumerate(terms):
        ext = jnp.where(lane == i, t, ext)
    return ext


def _fox_kernel(q_ref, k_ref, v_ref, c_ref, o_ref, kaug, vt, sc):
    tq = q_ref.shape[0]
    nblk = k_ref.shape[0] // tq
    qi = pl.program_id(1)
    hd = FOX_HEAD_DIM

    @pl.when(qi == 0)
    def _():
        for j in range(nblk):
            rows = slice(j * tq, (j + 1) * tq)
            kblk = k_ref[rows, :]
            cblk = c_ref[rows, :]
            for h in range(FOX_HEADS):
                kaug[h, rows, :] = jnp.concatenate(
                    [kblk[:, h * hd:(h + 1) * hd], _fox_bias_lanes(cblk[:, h:h + 1], False)], axis=1).astype(BF16)
            vt[j] = v_ref[rows, :].T.astype(BF16)

    qblk = q_ref[...] * (hd ** -0.5)
    cq = c_ref[pl.ds(pl.multiple_of(qi * tq, tq), tq), :]
    qaug = [jnp.concatenate([qblk[:, h * hd:(h + 1) * hd], _fox_bias_lanes(cq[:, h:h + 1], True)], axis=1).astype(BF16)
            for h in range(FOX_HEADS)]
    key = lax.broadcasted_iota(jnp.int32, (tq, tq), 0)
    qry = lax.broadcasted_iota(jnp.int32, (tq, tq), 1)
    causal = key <= qry
    nt = (((1,), (1,)), ((), ()))

    def score(j, stats, masked):
        start = pl.multiple_of(j * tq, tq)
        new = []
        for h in range(FOX_HEADS):
            m, l = stats[h]
            st = lax.dot_general(kaug[h, pl.ds(start, tq), :], qaug[h], nt, preferred_element_type=F32)
            if masked:
                st = jnp.where(causal, st, NEG)
            sc[h, j] = st
            m_new = jnp.maximum(m, jnp.max(st, axis=0, keepdims=True))
            l = jnp.exp(m - m_new) * l + jnp.sum(jnp.exp(st - m_new), axis=0, keepdims=True)
            new.append((m_new, l))
        return tuple(new)

    stats = tuple((jnp.full((1, tq), NEG, F32), jnp.zeros((1, tq), F32)) for _ in range(FOX_HEADS))
    stats = lax.fori_loop(0, qi, lambda j, c: score(j, c, False), stats)
    stats = score(qi, stats, True)

    def apply(j, accs):
        new = []
        for h in range(FOX_HEADS):
            m, l = stats[h]
            p = jnp.exp(sc[h, j] - m) / l
            new.append(accs[h] + jnp.dot(vt[j, h * hd:(h + 1) * hd, :], p.astype(BF16), preferred_element_type=F32))
        return tuple(new)

    accs = lax.fori_loop(0, qi + 1, apply, tuple(jnp.zeros((hd, tq), F32) for _ in range(FOX_HEADS)))
    o_ref[...] = jnp.concatenate(accs, axis=0).T


def _fox_prompt(h, c, nb, seq):
    tq = FOX_CH
    nq = seq // tq
    t = nb * seq
    return pl.pallas_call(
        _fox_kernel,
        grid=(nb, nq),
        in_specs=[pl.BlockSpec((tq, GROUP_W), lambda b, i: (b * nq + i, COL_Q // GROUP_W)),
                  pl.BlockSpec((seq, GROUP_W), lambda b, i: (b, COL_K // GROUP_W)),
                  pl.BlockSpec((seq, GROUP_W), lambda b, i: (b, COL_V // GROUP_W)),
                  pl.BlockSpec((seq, LANES), lambda b, i: (b, 0))],
        out_specs=pl.BlockSpec((tq, GROUP_W), lambda b, i: (b * nq + i, 0)),
        out_shape=jax.ShapeDtypeStruct((t, GROUP_W), F32),
        scratch_shapes=[pltpu.VMEM((FOX_HEADS, seq, 2 * FOX_HEAD_DIM), BF16),
                        pltpu.VMEM((nq, GROUP_W, tq), BF16),
                        pltpu.VMEM((FOX_HEADS, nq, tq, tq), F32)],
        compiler_params=_cp(("arbitrary", "arbitrary")),
        name="fox_prompt",
    )(h, h, h, c)


def _kv_state_kernel(k_ref, v_ref, *rest):
    ko_ref, vo_ref = rest[-2:]
    k = k_ref[...]
    v = v_ref[...]
    for h in range(FOX_HEADS):
        ko_ref[:, h, :] = k[:, h * FOX_HEAD_DIM:(h + 1) * FOX_HEAD_DIM]
        vo_ref[:, h, :] = v[:, h * FOX_HEAD_DIM:(h + 1) * FOX_HEAD_DIM]


def _kv_state(h, t, layer, depth, prev=None):
    tm = 512
    prev = () if prev is None else tuple(prev)
    osp = pl.BlockSpec((None, tm, FOX_HEADS, FOX_HEAD_DIM), lambda i: (layer, i, 0, 0))
    osd = jax.ShapeDtypeStruct((depth, t, FOX_HEADS, FOX_HEAD_DIM), F32)
    return pl.pallas_call(
        _kv_state_kernel,
        grid=(t // tm,),
        in_specs=[pl.BlockSpec((tm, GROUP_W), lambda i: (i, COL_K // GROUP_W)),
                  pl.BlockSpec((tm, GROUP_W), lambda i: (i, COL_V // GROUP_W))]
                 + [pl.BlockSpec(memory_space=pl.ANY)] * len(prev),
        out_specs=[osp, osp],
        out_shape=[osd, osd],
        input_output_aliases={2 + i: i for i in range(len(prev))},
        compiler_params=_cp(("arbitrary",)),
        name="kv_state",
    )(h, h, *prev)


DEC_PPS = 8


def _fox_dec_kernel(pt_ref, q_ref, kn_ref, vn_ref, lfn_ref, *rest):
    del pt_ref
    pps = DEC_PPS
    k_refs = rest[:pps]
    v_refs = rest[pps:2 * pps]
    lf_refs = rest[2 * pps:3 * pps]
    o_ref = rest[3 * pps]
    m_s, l_s, acc_s, carry_s, qm_s, snew_s, lg_s = rest[3 * pps + 1:]
    b = pl.program_id(0)
    s = pl.program_id(1)
    ns = pl.num_programs(1) // 2
    hrow = lax.broadcasted_iota(jnp.int32, (FOX_HEADS, GROUP_W), 0)
    hcol = lax.broadcasted_iota(jnp.int32, (FOX_HEADS, GROUP_W), 1) // FOX_HEAD_DIM
    headmask = hrow == hcol
    nt = (((1,), (1,)), ((), ()))

    @pl.when(s == 0)
    def _():
        qrow = q_ref[pl.ds(b, 1), :] * (FOX_HEAD_DIM ** -0.5)
        qm = jnp.where(headmask, jnp.broadcast_to(qrow, (FOX_HEADS, GROUP_W)), 0.0)
        qm_s[...] = qm.astype(BF16)
        kn = kn_ref[pl.ds(b, 1), :].astype(BF16).astype(F32)
        snew = jnp.sum(qm.astype(BF16).astype(F32) * kn, axis=-1, keepdims=True)
        snew_s[...] = snew
        m_s[...] = snew
        lfn = jnp.broadcast_to(lfn_ref[pl.ds(b, 1), :], (FOX_HEADS, LANES))
        er = lax.broadcasted_iota(jnp.int32, (FOX_HEADS, LANES), 0)
        ec = lax.broadcasted_iota(jnp.int32, (FOX_HEADS, LANES), 1)
        carry_s[...] = jnp.sum(jnp.where(er == ec, lfn, 0.0), axis=-1, keepdims=True)

    @pl.when(s < ns)
    def _():
        r = lax.broadcasted_iota(jnp.int32, (PAGE_SIZE, PAGE_SIZE), 0)
        c = lax.broadcasted_iota(jnp.int32, (PAGE_SIZE, PAGE_SIZE), 1)
        strict = jnp.where(r > c, 1.0, 0.0).astype(BF16)
        qm = qm_s[...]
        m, carry = m_s[...], carry_s[...]
        for i in range(pps):
            kp = k_refs[i][...].astype(BF16)
            lf = lf_refs[i][...]
            st = lax.dot_general(qm, kp, nt, preferred_element_type=F32)
            logits = st + (_dot01_right(lf, strict) + carry)
            carry = carry + jnp.sum(lf, axis=-1, keepdims=True)
            m = jnp.maximum(m, jnp.max(logits, axis=-1, keepdims=True))
            lg_s[s * pps + i] = logits
        m_s[...], carry_s[...] = m, carry

    @pl.when(s == ns - 1)
    def _():
        m = m_s[...]

        def add(j, l):
            return l + jnp.sum(jnp.exp(lg_s[j] - m), axis=-1, keepdims=True)

        l_s[...] = lax.fori_loop(0, ns * pps, add, jnp.exp(snew_s[...] - m))
        acc_s[...] = jnp.zeros_like(acc_s)

    @pl.when(s >= ns)
    def _():
        m, l = m_s[...], l_s[...]
        acc = acc_s[...]
        for i in range(pps):
            vp = v_refs[i][...].astype(BF16)
            p = jnp.exp(lg_s[(s - ns) * pps + i] - m) / l
            acc = acc + jnp.dot(p.astype(BF16), vp, preferred_element_type=F32)
        acc_s[...] = acc

    @pl.when(s == 2 * ns - 1)
    def _():
        m, l = m_s[...], l_s[...]
        pn = (jnp.exp(snew_s[...] - m) / l).astype(BF16).astype(F32)
        vn = jnp.broadcast_to(vn_ref[pl.ds(b, 1), :].astype(BF16).astype(F32), (FOX_HEADS, GROUP_W))
        o_ref[...] = jnp.sum(jnp.where(headmask, acc_s[...] + pn * vn, 0.0), axis=0, keepdims=True)


def _fox_decode(hs, rb, lfn, ck, cv, clft, page_table, layer):
    nb, npg = page_table.shape
    pps = DEC_PPS
    ns = npg // pps
    pt = page_table.reshape(-1)

    def page_map(i, phase):
        def index(b, s, pt_ref):
            sp = jnp.minimum(s, ns - 1) if phase == 0 else jnp.maximum(s - ns, 0)
            return (layer, pt_ref[b * npg + (npg - 1 - (sp * pps + i))], 0, 0)
        return index

    full = lambda cb: pl.BlockSpec((nb, GROUP_W), lambda b, s, pt_ref: (rb, cb))
    in_specs = [full(COL_Q // GROUP_W), full(COL_K // GROUP_W), full(COL_V // GROUP_W),
                pl.BlockSpec((nb, LANES), lambda b, s, pt_ref: (0, 0))]
    in_specs += [pl.BlockSpec((None, None, PAGE_SIZE, GROUP_W), page_map(i, 0)) for i in range(pps)]
    in_specs += [pl.BlockSpec((None, None, PAGE_SIZE, GROUP_W), page_map(i, 1)) for i in range(pps)]
    in_specs += [pl.BlockSpec((None, None, FOX_HEADS, PAGE_SIZE), page_map(i, 0)) for i in range(pps)]
    out = pl.pallas_call(
        _fox_dec_kernel,
        grid_spec=pltpu.PrefetchScalarGridSpec(
            num_scalar_prefetch=1,
            grid=(nb, 2 * ns),
            in_specs=in_specs,
            out_specs=pl.BlockSpec((None, 1, GROUP_W), lambda b, s, pt_ref: (b, 0, 0)),
            scratch_shapes=[pltpu.VMEM((FOX_HEADS, 1), F32), pltpu.VMEM((FOX_HEADS, 1), F32),
                            pltpu.VMEM((FOX_HEADS, GROUP_W), F32), pltpu.VMEM((FOX_HEADS, 1), F32),
                            pltpu.VMEM((FOX_HEADS, GROUP_W), BF16), pltpu.VMEM((FOX_HEADS, 1), F32),
                            pltpu.VMEM((npg, FOX_HEADS, PAGE_SIZE), F32)]),
        out_shape=jax.ShapeDtypeStruct((nb, 1, GROUP_W), F32),
        compiler_params=_cp(("arbitrary", "arbitrary")),
        name="fox_decode",
    )(pt, hs, hs, hs, lfn, *([ck] * pps), *([cv] * pps), *([clft] * pps))
    return out.reshape(nb, GROUP_W)


def _ssd_kernel(xbc_ref, z_ref, misc_ref, cw_ref, cb_ref, hp_ref, o_ref, st_ref, cs_ref, carry, hstate):
    q = SSD_CHUNK
    c = pl.program_id(1)
    nc = pl.num_programs(1)

    @pl.when(c == 0)
    def _():
        carry[...] = jnp.zeros_like(carry)
        hstate[...] = jnp.zeros_like(hstate)

    xin = xbc_ref[...]
    full = jnp.concatenate([carry[...], xin], axis=0).astype(BF16).astype(F32)
    cw = cw_ref[...].astype(BF16).astype(F32)
    conv = cb_ref[...] + sum(full[SUBLANES - (SSD_CONV - 1) + w:SUBLANES - (SSD_CONV - 1) + w + q, :] * cw[w:w + 1, :]
                             for w in range(SSD_CONV))
    carry[...] = xin[q - SUBLANES:q, :]
    cs_ref[...] = xin[q - SUBLANES:q, :]
    xc = _silu(conv)
    xs = xc[:, :GROUP_W]

    dt = _softplus(misc_ref[...] + hp_ref[0:1, :])
    a_row = -jnp.exp(hp_ref[1:2, :])
    d_row = hp_ref[2:3, :]
    da = dt * a_row
    row = lax.broadcasted_iota(jnp.int32, (q, q), 0)
    col = lax.broadcasted_iota(jnp.int32, (q, q), 1)
    tril = row >= col
    tril01 = jnp.where(tril, 1.0, 0.0).astype(BF16)
    triu01 = jnp.where(row <= col, 1.0, 0.0).astype(BF16)
    acs = _dot01_left(tril01, da)
    dt_t = dt.T
    acs_t = _dot01_right(da.T, triu01)
    nt = (((1,), (1,)), ((), ()))
    tn = (((0,), (0,)), ((), ()))
    rep = SSD_HEADS // SSD_GROUPS
    for g in range(SSD_GROUPS):
        bm = xc[:, GROUP_W + g * SSD_STATE:GROUP_W + (g + 1) * SSD_STATE].astype(BF16)
        cm = xc[:, GROUP_W + (SSD_GROUPS + g) * SSD_STATE:GROUP_W + (SSD_GROUPS + g + 1) * SSD_STATE].astype(BF16)
        cbm = lax.dot_general(cm, bm, nt, preferred_element_type=F32)
        for r in range(rep):
            hd = g * rep + r
            ln = MISC_DT + hd
            lo, hi = hd * SSD_HEAD_DIM, (hd + 1) * SSD_HEAD_DIM
            a_col = acs[:, ln:ln + 1]
            a_rw = acs_t[ln:ln + 1, :]
            dt_rw = dt_t[ln:ln + 1, :]
            dt_col = dt[:, ln:ln + 1]
            a_last = acs[q - 1:q, ln:ln + 1]
            decay = jnp.where(tril, jnp.exp(jnp.minimum(a_col - a_rw, 0.0)), 0.0)
            mm = (cbm * decay * dt_rw).astype(BF16)
            xh = xs[:, lo:hi]
            y = jnp.dot(mm, xh.astype(BF16), preferred_element_type=F32)
            xw = (xh * (jnp.exp(a_last - a_col) * dt_col)).astype(BF16)
            states = lax.dot_general(xw, bm, tn, preferred_element_type=F32)
            hprev = hstate[hd]
            y_off = lax.dot_general(cm, hprev.astype(BF16), nt, preferred_element_type=F32)
            y = y + y_off * jnp.exp(a_col) + d_row[:, ln:ln + 1] * xh
            hstate[hd] = hprev * jnp.exp(a_last) + states
            o_ref[:, lo:hi] = y
    o_ref[...] = o_ref[...] * _silu(z_ref[...])

    @pl.when(c == nc - 1)
    def _():
        st_ref[...] = hstate[...]


def _ssd_prompt(h, cw, cb, hp, nb, seq):
    q = SSD_CHUNK
    nc = seq // q
    t = nb * seq
    return pl.pallas_call(
        _ssd_kernel,
        grid=(nb, nc),
        in_specs=[pl.BlockSpec((q, SSD_CONV_DIM), lambda b, c: (b * nc + c, COL_XBC // SSD_CONV_DIM)),
                  pl.BlockSpec((q, GROUP_W), lambda b, c: (b * nc + c, COL_Z // GROUP_W)),
                  pl.BlockSpec((q, LANES), lambda b, c: (b * nc + c, COL_MISC // LANES)),
                  pl.BlockSpec((SSD_CONV, SSD_CONV_DIM), lambda b, c: (0, 0)),
                  pl.BlockSpec((1, SSD_CONV_DIM), lambda b, c: (0, 0)),
                  pl.BlockSpec((SUBLANES, LANES), lambda b, c: (0, 0))],
        out_specs=[pl.BlockSpec((q, GROUP_W), lambda b, c: (b * nc + c, 0)),
                   pl.BlockSpec((None, SSD_HEADS, SSD_HEAD_DIM, SSD_STATE), lambda b, c: (b, 0, 0, 0)),
                   pl.BlockSpec((None, SUBLANES, SSD_CONV_DIM), lambda b, c: (b, 0, 0))],
        out_shape=[jax.ShapeDtypeStruct((t, GROUP_W), F32),
                   jax.ShapeDtypeStruct((nb, SSD_HEADS, SSD_HEAD_DIM, SSD_STATE), F32),
                   jax.ShapeDtypeStruct((nb, SUBLANES, SSD_CONV_DIM), F32)],
        scratch_shapes=[pltpu.VMEM((SUBLANES, SSD_CONV_DIM), F32),
                        pltpu.VMEM((SSD_HEADS, SSD_HEAD_DIM, SSD_STATE), F32)],
        compiler_params=_cp(("arbitrary", "arbitrary")),
        name="ssd_prompt",
    )(h, h, h, cw, cb, hp)


def _gmlp_kernel(u_ref, v_ref, lng_ref, lnb_ref, ws_ref, bst_ref, o_ref):
    n = GMLP_CHUNK
    v = _layer_norm(_gelu(v_ref[...]), lng_ref[...], lnb_ref[...])
    row = lax.broadcasted_iota(jnp.int32, (n, n), 0)
    col = lax.broadcasted_iota(jnp.int32, (n, n), 1)
    tril = row >= col
    gc = GROUP_W // GMLP_GROUPS
    parts = []
    for g in range(GMLP_GROUPS):
        wm = jnp.where(tril, ws_ref[g], 0.0).astype(BF16)
        sg = jnp.dot(wm, v[:, g * gc:(g + 1) * gc].astype(BF16), preferred_element_type=F32)
        parts.append(sg + bst_ref[:, g:g + 1])
    s = jnp.concatenate(parts, axis=-1)
    o_ref[...] = _gelu(u_ref[...]) * s


def _gmlp_prompt(h, lng, lnb, ws, bst, t):
    n = GMLP_CHUNK
    return pl.pallas_call(
        _gmlp_kernel,
        grid=(t // n,),
        in_specs=[pl.BlockSpec((n, GROUP_W), lambda i: (i, COL_GU // GROUP_W)),
                  pl.BlockSpec((n, GROUP_W), lambda i: (i, COL_GV // GROUP_W)),
                  pl.BlockSpec((1, GROUP_W), lambda i: (0, 0)),
                  pl.BlockSpec((1, GROUP_W), lambda i: (0, 0)),
                  pl.BlockSpec((GMLP_GROUPS, n, n), lambda i: (0, 0, 0)),
                  pl.BlockSpec((n, LANES), lambda i: (0, 0))],
        out_specs=pl.BlockSpec((n, GROUP_W), lambda i: (i, 0)),
        out_shape=jax.ShapeDtypeStruct((t, GROUP_W), F32),
        compiler_params=_cp(("arbitrary",)),
        name="gmlp_prompt",
    )(h, h, lng, lnb, ws, bst)


CONF_HALO = 32
CONF_CHUNK = 128


def _conf_kernel(a_ref, g_ref, cw_ref, cb_ref, lng_ref, lnb_ref, o_ref, st_ref, carry):
    n = CONF_CHUNK

    @pl.when(pl.program_id(1) == 0)
    def _():
        carry[...] = jnp.zeros_like(carry)

    glu = a_ref[...] * _sigmoid(g_ref[...])
    full = jnp.concatenate([carry[...], glu], axis=0).astype(BF16).astype(F32)
    cw = cw_ref[...].astype(BF16).astype(F32)
    off = CONF_HALO - (CONF_KERNEL - 1)
    acc = cb_ref[...] + full[off:off + n, :] * cw[0:1, :]
    for w in range(1, CONF_KERNEL):
        acc = acc + full[off + w:off + w + n, :] * cw[w:w + 1, :]
    carry[...] = glu[n - CONF_HALO:n, :]
    st_ref[...] = glu[n - CONF_HALO:n, :]
    o_ref[...] = _silu(_layer_norm(acc, lng_ref[...], lnb_ref[...]))


def _conf_prompt(h, cw, cb, lng, lnb, nb, seq):
    n = CONF_CHUNK
    nc = seq // n
    t = nb * seq
    return pl.pallas_call(
        _conf_kernel,
        grid=(nb, nc),
        in_specs=[pl.BlockSpec((n, GROUP_W), lambda b, c: (b * nc + c, COL_GA // GROUP_W)),
                  pl.BlockSpec((n, GROUP_W), lambda b, c: (b * nc + c, COL_GB // GROUP_W)),
                  pl.BlockSpec((CONF_HALO, GROUP_W), lambda b, c: (0, 0)),
                  pl.BlockSpec((1, GROUP_W), lambda b, c: (0, 0)),
                  pl.BlockSpec((1, GROUP_W), lambda b, c: (0, 0)),
                  pl.BlockSpec((1, GROUP_W), lambda b, c: (0, 0))],
        out_specs=[pl.BlockSpec((n, GROUP_W), lambda b, c: (b * nc + c, 0)),
                   pl.BlockSpec((None, CONF_HALO, GROUP_W), lambda b, c: (b, 0, 0))],
        out_shape=[jax.ShapeDtypeStruct((t, GROUP_W), F32),
                   jax.ShapeDtypeStruct((nb, CONF_HALO, GROUP_W), F32)],
        scratch_shapes=[pltpu.VMEM((CONF_HALO, GROUP_W), F32)],
        compiler_params=_cp(("arbitrary", "arbitrary")),
        name="conf_prompt",
    )(h, h, cw, cb, lng, lnb)


def _dec_rows_kernel(gu_ref, gv_ref, ga_ref, gb_ref, xbc_ref, misc_ref,
                     lng_ref, lnb_ref, w00_ref, b0_ref,
                     ccw_ref, ccb_ref, clg_ref, clb_ref, cprev_ref,
                     scw_ref, scb_ref, sprev_ref, hp_ref,
                     c_ref, vrow_ref, d_ref, glu_ref, xc_ref, sm_ref):
    nb = gu_ref.shape[0]
    v = _layer_norm(_gelu(gv_ref[...]), lng_ref[...], lnb_ref[...])
    vrow_ref[...] = v
    c_ref[...] = _gelu(gu_ref[...]) * (w00_ref[...] * v + b0_ref[...])
    glu = ga_ref[...] * _sigmoid(gb_ref[...])
    glu_ref[...] = glu
    kw = CONF_KERNEL - 1
    rows = []
    for b in range(nb):
        rows.append(jnp.sum(cprev_ref[b] * ccw_ref[0:kw, :], axis=0, keepdims=True))
    conv = jnp.concatenate(rows, axis=0) + glu * ccw_ref[kw:kw + 1, :] + ccb_ref[...]
    d_ref[...] = _silu(_layer_norm(conv, clg_ref[...], clb_ref[...]))
    sk = SSD_CONV - 1
    rows = []
    for b in range(nb):
        rows.append(jnp.sum(sprev_ref[b] * scw_ref[0:sk, :], axis=0, keepdims=True))
    sconv = jnp.concatenate(rows, axis=0) + xbc_ref[...] * scw_ref[sk:sk + 1, :] + scb_ref[...]
    xc_ref[...] = _silu(sconv)
    misc = misc_ref[...]
    lane = lax.broadcasted_iota(jnp.int32, misc.shape, 1)
    lf = _log_sigmoid(misc + hp_ref[3:4, :])
    dt = _softplus(misc + hp_ref[0:1, :])
    sm_ref[...] = jnp.where(lane < MISC_DT, lf, dt)


def _dec_rows(hs, rb, nb, lng, lnb, w00, b0, ccw, ccb, clg, clb, cprev, scw, scb, sprev, hp):
    col = lambda c0, w: pl.BlockSpec((nb, w), lambda i: (rb, c0 // w))
    whole = lambda a: pl.BlockSpec(a.shape, lambda i: (0,) * a.ndim)
    sd = lambda w: jax.ShapeDtypeStruct((nb, w), F32)
    return pl.pallas_call(
        _dec_rows_kernel,
        grid=(1,),
        in_specs=[col(COL_GU, GROUP_W), col(COL_GV, GROUP_W), col(COL_GA, GROUP_W), col(COL_GB, GROUP_W),
                  col(COL_XBC, SSD_CONV_DIM), col(COL_MISC, LANES)]
                 + [whole(a) for a in (lng, lnb, w00, b0, ccw, ccb, clg, clb, cprev, scw, scb, sprev, hp)],
        out_specs=[pl.BlockSpec((nb, GROUP_W), lambda i: (0, 0))] * 4
                  + [pl.BlockSpec((nb, SSD_CONV_DIM), lambda i: (0, 0)), pl.BlockSpec((nb, LANES), lambda i: (0, 0))],
        out_shape=[sd(GROUP_W)] * 4 + [sd(SSD_CONV_DIM), sd(LANES)],
        compiler_params=_cp(("arbitrary",)),
        name="dec_rows",
    )(hs, hs, hs, hs, hs, hs, lng, lnb, w00, b0, ccw, ccb, clg, clb, cprev, scw, scb, sprev, hp)


def _dec_ssm_kernel(xt_ref, zt_ref, bc_ref, sm_ref, hp_ref, h_ref, yt_ref, hn_ref):
    nb = bc_ref.shape[0]
    rep = SSD_HEADS // SSD_GROUPS
    a_row = -jnp.exp(hp_ref[1:2, :])
    d_row = hp_ref[2:3, :]
    sm = sm_ref[...]
    for b in range(nb):
        cols = []
        for hd in range(SSD_HEADS):
            g = hd // rep
            ln = MISC_DT + hd
            lo, hi = hd * SSD_HEAD_DIM, (hd + 1) * SSD_HEAD_DIM
            dt = sm[b:b + 1, ln:ln + 1]
            x = xt_ref[lo:hi, b:b + 1]
            brow = bc_ref[b:b + 1, g * SSD_STATE:(g + 1) * SSD_STATE]
            crow = bc_ref[b:b + 1, (SSD_GROUPS + g) * SSD_STATE:(SSD_GROUPS + g + 1) * SSD_STATE]
            hnew = h_ref[b, hd] * jnp.exp(dt * a_row[:, ln:ln + 1]) + (dt * x) * brow
            hn_ref[b, hd] = hnew
            y = jnp.sum(hnew.astype(BF16).astype(F32) * crow.astype(BF16).astype(F32), axis=-1, keepdims=True)
            cols.append(y + d_row[:, ln:ln + 1] * x)
        ycol = jnp.concatenate(cols, axis=0)
        yt_ref[:, b:b + 1] = ycol * _silu(zt_ref[:, b:b + 1])


def _dec_ssm(xt, zt, bc, sm, hp, hstate):
    nb = bc.shape[0]
    whole = lambda a: pl.BlockSpec(a.shape, lambda i: (0,) * a.ndim)
    return pl.pallas_call(
        _dec_ssm_kernel,
        grid=(1,),
        in_specs=[whole(a) for a in (xt, zt, bc, sm, hp, hstate)],
        out_specs=[pl.BlockSpec((GROUP_W, nb), lambda i: (0, 0)),
                   pl.BlockSpec(hstate.shape, lambda i: (0, 0, 0, 0))],
        out_shape=[jax.ShapeDtypeStruct((GROUP_W, nb), F32), jax.ShapeDtypeStruct(hstate.shape, F32)],
        compiler_params=_cp(("arbitrary",)),
        name="dec_ssm",
    )(xt, zt, bc, sm, hp, hstate)


def _rows_to_tiles(pieces):
    return jnp.swapaxes(jnp.stack(pieces, axis=0), 0, 1)


def _tiles_to_rows(block):
    t = jnp.swapaxes(block, 0, 1)
    return [t[s] for s in range(SUBLANES)]


def _pack_bf16_pair(lo, hi):
    lo_bits = lax.bitcast_convert_type(lo.astype(BF16).astype(F32), jnp.uint32)
    hi_bits = lax.bitcast_convert_type(hi.astype(BF16).astype(F32), jnp.uint32)
    return (lo_bits >> 16) | (hi_bits & jnp.uint32(0xFFFF0000))


def _unpack_bf16_pair(w):
    lo = lax.bitcast_convert_type(w << 16, F32).astype(BF16)
    hi = lax.bitcast_convert_type(w & jnp.uint32(0xFFFF0000), F32).astype(BF16)
    return lo, hi


def _outproj_kernel(a_ref, b_ref, c_ref, d_ref, x_ref, w_ref, mg_ref, g_ref, bb_ref, rw_ref, rb_ref, *rest):
    h_ref, hb_ref, ti_ref, tg_ref = rest[-4:]
    y = None
    for i, m_ref in enumerate((a_ref, b_ref, c_ref, d_ref)):
        m = m_ref[...]
        r = lax.rsqrt(jnp.mean(m * m, axis=-1, keepdims=True) + EPS)
        nrm = (m * r * mg_ref[:, i * GROUP_W:(i + 1) * GROUP_W]).astype(BF16)
        part = jnp.dot(nrm, w_ref[i * GROUP_W:(i + 1) * GROUP_W, :], preferred_element_type=F32)
        y = part if y is None else y + part
    h = _layer_norm(DEEPNORM_ALPHA * x_ref[...] + y, g_ref[...], bb_ref[...])
    h_ref[...] = h
    packed = _pack_bf16_pair(h[:, :D_MODEL // 2], h[:, D_MODEL // 2:])
    for i in range(h.shape[0] // SUBLANES):
        rows = slice(i * SUBLANES, (i + 1) * SUBLANES)
        hb_ref[rows] = _rows_to_tiles([packed[rows, s * LANES:(s + 1) * LANES] for s in range(ROW_TILES_X)])
    lg = jnp.dot(h.astype(BF16), rw_ref[...].astype(BF16), preferred_element_type=F32) + rb_ref[...]
    lane = lax.broadcasted_iota(jnp.int32, lg.shape, 1)
    ti = jnp.zeros(lg.shape, jnp.int32)
    tv = jnp.full(lg.shape, NEG, F32)
    for k in range(TOP_K):
        mx = jnp.max(lg, axis=-1, keepdims=True)
        idx = jnp.min(jnp.where(lg == mx, lane, LANES), axis=-1, keepdims=True)
        ti = jnp.where(lane == k, idx, ti)
        tv = jnp.where(lane == k, mx, tv)
        lg = jnp.where(lane == idx, NEG, lg)
    e = jnp.exp(tv - jnp.max(tv, axis=-1, keepdims=True))
    tg_ref[...] = e / jnp.sum(e, axis=-1, keepdims=True)
    ti_ref[...] = ti


def _outproj(a, b, c, d, x_all, w, mg, g, bb, rw, rb, tm, row0, prev=None):
    t_all = x_all.shape[0]
    n = a.shape[0]
    off = row0 // tm
    row = lambda w_: pl.BlockSpec((tm, w_), lambda i: (i, 0))
    row_off = lambda w_: pl.BlockSpec((tm, w_), lambda i: (i + off, 0))
    whole = lambda arr: pl.BlockSpec(arr.shape, lambda i: (0,) * arr.ndim)
    prev = () if prev is None else tuple(prev)
    n_in = 11
    return pl.pallas_call(
        _outproj_kernel,
        grid=(n // tm,),
        in_specs=[row(GROUP_W)] * 4 + [row_off(D_MODEL)] + [whole(arr) for arr in (w, mg, g, bb, rw, rb)]
                 + [pl.BlockSpec(memory_space=pl.ANY)] * len(prev),
        out_specs=[row_off(D_MODEL), pl.BlockSpec((tm, ROW_TILES_X, LANES), lambda i: (i + off, 0, 0)),
                   row_off(LANES), row_off(LANES)],
        out_shape=[jax.ShapeDtypeStruct((t_all, D_MODEL), F32),
                   jax.ShapeDtypeStruct((t_all, ROW_TILES_X, LANES), jnp.uint32),
                   jax.ShapeDtypeStruct((t_all, LANES), jnp.int32), jax.ShapeDtypeStruct((t_all, LANES), F32)],
        input_output_aliases={n_in + i: i for i in range(len(prev))},
        compiler_params=_cp(("arbitrary",)),
        name="outproj_ln_router",
    )(a, b, c, d, x_all, w, mg, g, bb, rw, rb, *prev)


def _moe_kernel(ge_ref, gb_ref, gr_ref, x_ref, wg_ref, wu_ref, wd_ref, bg_ref, bu_ref, bd_ref, y_ref,
                xb, yacc, wgu, wdb):
    del ge_ref, gb_ref
    g = pl.program_id(0)
    f = pl.program_id(1)
    nf = pl.num_programs(1)
    rows = gr_ref[g]
    half = D_MODEL // 2

    @pl.when(rows > 0)
    def _():
        @pl.when(f == 0)
        def _():
            def unpack16(i, carry):
                r0 = pl.multiple_of(i * 2 * SUBLANES, 2 * SUBLANES)
                top = _tiles_to_rows(x_ref[pl.ds(r0, SUBLANES)])
                bot = _tiles_to_rows(x_ref[pl.ds(r0 + SUBLANES, SUBLANES)])
                for s in range(ROW_TILES_X):
                    lo, hi = _unpack_bf16_pair(jnp.concatenate([top[s], bot[s]], axis=0))
                    xb[pl.ds(r0, 2 * SUBLANES), s * LANES:(s + 1) * LANES] = lo
                    xb[pl.ds(r0, 2 * SUBLANES), half + s * LANES:half + (s + 1) * LANES] = hi
                return carry

            lax.fori_loop(0, MOE_R // (2 * SUBLANES), unpack16, 0, unroll=2)
            yacc[...] = jnp.broadcast_to(bd_ref[...], yacc.shape)

        wgu[:, :MOE_TF] = wg_ref[...].astype(BF16)
        wgu[:, MOE_TF:] = wu_ref[...].astype(BF16)
        wdb[...] = wd_ref[...].astype(BF16)
        hgu = jnp.dot(xb[...], wgu[...], preferred_element_type=F32)
        gate = jnp.minimum(hgu[:, :MOE_TF] + bg_ref[...], SWIGLU_LIMIT)
        up = jnp.clip(hgu[:, MOE_TF:] + bu_ref[...], -SWIGLU_LIMIT, SWIGLU_LIMIT)
        act = (up + 1.0) * (gate * _sigmoid(SWIGLU_ALPHA * gate))
        yacc[...] += jnp.dot(act.astype(BF16), wdb[...], preferred_element_type=F32)

        @pl.when(f == nf - 1)
        def _():
            def to_tiles(i, carry):
                r0 = pl.multiple_of(i * SUBLANES, SUBLANES)
                for t0 in range(0, ROW_TILES_Y, SUBLANES):
                    y_ref[pl.ds(r0, SUBLANES), t0:t0 + SUBLANES, :] = _rows_to_tiles(
                        [yacc[pl.ds(r0, SUBLANES), (t0 + s) * LANES:(t0 + s + 1) * LANES] for s in range(SUBLANES)])
                return carry

            lax.fori_loop(0, MOE_R // SUBLANES, to_tiles, 0, unroll=2)


def _moe_ffn(x_rows, w_gu, b_gu, w_dn, b_dn, layer, ge, gb, gr):
    ng = ge.shape[0]
    d = D_MODEL
    nf = D_FF // MOE_TF
    r = MOE_R
    depth = w_gu.shape[0]

    def fidx(g, f, gr_ref):
        return jnp.where(gr_ref[g] > 0, f, nf - 1)

    in_specs = [
        pl.BlockSpec((r, ROW_TILES_X, LANES), lambda g, f, ge_, gb_, gr_: (gb_[g], 0, 0)),
        pl.BlockSpec((None, None, d, MOE_TF), lambda g, f, ge_, gb_, gr_: (layer, ge_[g], 0, fidx(g, f, gr_))),
        pl.BlockSpec((None, None, d, MOE_TF), lambda g, f, ge_, gb_, gr_: (layer, ge_[g], 0, nf + fidx(g, f, gr_))),
        pl.BlockSpec((None, None, MOE_TF, d), lambda g, f, ge_, gb_, gr_: (layer, ge_[g], fidx(g, f, gr_), 0)),
        pl.BlockSpec((None, None, 1, MOE_TF), lambda g, f, ge_, gb_, gr_: (layer, ge_[g], 0, fidx(g, f, gr_))),
        pl.BlockSpec((None, None, 1, MOE_TF), lambda g, f, ge_, gb_, gr_: (layer, ge_[g], 0, nf + fidx(g, f, gr_))),
        pl.BlockSpec((None, None, 1, d), lambda g, f, ge_, gb_, gr_: (layer, ge_[g], 0, 0)),
    ]
    b_gu4 = b_gu.reshape(depth, N_EXPERTS, 1, -1)
    return pl.pallas_call(
        _moe_kernel,
        grid_spec=pltpu.PrefetchScalarGridSpec(
            num_scalar_prefetch=3,
            grid=(ng, nf),
            in_specs=in_specs,
            out_specs=pl.BlockSpec((r, ROW_TILES_Y, LANES), lambda g, f, ge_, gb_, gr_: (gb_[g], 0, 0),
                                   pipeline_mode=pl.Buffered(1)),
            scratch_shapes=[pltpu.VMEM((r, d), BF16), pltpu.VMEM((r, d), F32), pltpu.VMEM((d, 2 * MOE_TF), BF16),
                            pltpu.VMEM((MOE_TF, d), BF16)]),
        out_shape=jax.ShapeDtypeStruct((ng * r, ROW_TILES_Y, LANES), F32),
        compiler_params=_cp(("arbitrary", "arbitrary")),
        name="moe_ffn",
    )(ge, gb, gr, x_rows, w_gu, w_gu, w_dn, b_gu4, b_gu4, b_dn.reshape(depth, N_EXPERTS, 1, -1))


def _moe_plan(top_i, n_tok):
    n_pairs = n_tok * TOP_K
    r = MOE_R
    ng = n_pairs // r + N_EXPERTS
    flat_e = top_i.reshape(-1)
    onehot = (flat_e[:, None] == jnp.arange(N_EXPERTS, dtype=jnp.int32)[None, :]).astype(jnp.int32)
    csum = jnp.cumsum(onehot, axis=0)
    counts = csum[-1]
    rank = jnp.sum(onehot * csum, axis=1) - 1
    ngrp = (counts + r - 1) // r
    grp_end = jnp.cumsum(ngrp)
    grp_start = grp_end - ngrp
    start = jnp.cumsum(counts) - counts
    dest = jnp.sum(onehot * grp_start[None, :], axis=1) * r + rank
    sorted_tok = (jnp.argsort(flat_e, stable=True) // TOP_K).astype(jnp.int32)
    total = grp_end[-1]
    gidx = jnp.arange(ng, dtype=jnp.int32)
    valid = gidx < total
    gclamp = jnp.minimum(gidx, total - 1)
    ge = jnp.sum(grp_end[None, :] <= gclamp[:, None], axis=1).astype(jnp.int32)
    local = gclamp - grp_start[ge]
    gr = jnp.where(valid, jnp.clip(counts[ge] - local * r, 0, r), 0).astype(jnp.int32)
    within = local[:, None] * r + jnp.arange(r, dtype=jnp.int32)[None, :]
    src = jnp.clip(start[ge][:, None] + within, 0, n_pairs - 1)
    row_tok = jnp.where(valid[:, None] & (within < counts[ge][:, None]), sorted_tok[src], 0)
    return row_tok.astype(jnp.int32), dest.reshape(n_tok, TOP_K).astype(jnp.int32), ge, gclamp.astype(jnp.int32), gr, total


def _gather_kernel(cnt_ref, idx_ref, src, out_ref, idx_s, sem_i, sem_g):
    g = pl.program_id(0)
    n = cnt_ref[0]
    nr, nc = idx_s.shape

    @pl.when(g < n)
    def _():
        cp = pltpu.make_async_copy(idx_ref, idx_s, sem_i)
        cp.start()
        cp.wait()
        for r in range(nr):
            def body(c, carry, r=r):
                pltpu.make_async_copy(src.at[pl.ds(idx_s[r, c], 1)], out_ref.at[pl.ds(r * nc + c, 1)], sem_g).start()
                return carry
            lax.fori_loop(0, nc, body, 0, unroll=8)
        pltpu.make_async_copy(out_ref, out_ref, sem_g).wait()


def _gather_rows(src, idx, cnt, ch):
    p = idx.shape[0]
    steps = p // ch
    nr = SUBLANES
    nc = ch // nr
    return pl.pallas_call(
        _gather_kernel,
        grid_spec=pltpu.PrefetchScalarGridSpec(
            num_scalar_prefetch=1,
            grid=(steps,),
            in_specs=[pl.BlockSpec((nr, nc), lambda g, c: (g, 0)), pl.BlockSpec(memory_space=pl.ANY)],
            out_specs=pl.BlockSpec((ch,) + src.shape[1:], lambda g, c: (jnp.minimum(g, c[0] - 1), 0, 0)),
            scratch_shapes=[pltpu.SMEM((nr, nc), jnp.int32), pltpu.SemaphoreType.DMA, pltpu.SemaphoreType.DMA]),
        out_shape=jax.ShapeDtypeStruct((p,) + src.shape[1:], src.dtype),
        compiler_params=_cp(("arbitrary",)),
        name="row_gather",
    )(cnt, idx.reshape(steps * nr, nc), src)


def _combine_kernel(y0_ref, y1_ref, y2_ref, y3_ref, tg_ref, h_ref, g_ref, b_ref, o_ref, ffn):
    y_refs = (y0_ref, y1_ref, y2_ref, y3_ref)

    def rows8(i, carry):
        r0 = pl.multiple_of(i * SUBLANES, SUBLANES)
        tg = tg_ref[pl.ds(r0, SUBLANES), :]
        for t0 in range(0, ROW_TILES_Y, SUBLANES):
            acc = None
            for k, y_ref in enumerate(y_refs):
                pieces = _tiles_to_rows(y_ref[pl.ds(r0, SUBLANES), t0:t0 + SUBLANES, :])
                gk = tg[:, k:k + 1]
                acc = [gk * p for p in pieces] if acc is None else [a + gk * p for a, p in zip(acc, pieces)]
            for s in range(SUBLANES):
                ffn[pl.ds(r0, SUBLANES), (t0 + s) * LANES:(t0 + s + 1) * LANES] = acc[s]
        return carry

    lax.fori_loop(0, h_ref.shape[0] // SUBLANES, rows8, 0, unroll=2)
    o_ref[...] = _layer_norm(DEEPNORM_ALPHA * h_ref[...] + ffn[...], g_ref[...], b_ref[...])


def _combine(yg, tg, h, g, b, tm):
    t, d = h.shape
    ysp = lambda k: pl.BlockSpec((None, tm, ROW_TILES_Y, LANES), lambda i: (k, i, 0, 0))
    return pl.pallas_call(
        _combine_kernel,
        grid=(pl.cdiv(t, tm),),
        in_specs=[ysp(0), ysp(1), ysp(2), ysp(3),
                  pl.BlockSpec((tm, LANES), lambda i: (i, 0)),
                  pl.BlockSpec((tm, d), lambda i: (i, 0)),
                  pl.BlockSpec((1, d), lambda i: (0, 0)),
                  pl.BlockSpec((1, d), lambda i: (0, 0))],
        out_specs=pl.BlockSpec((tm, d), lambda i: (i, 0)),
        out_shape=jax.ShapeDtypeStruct((t, d), F32),
        scratch_shapes=[pltpu.VMEM((tm, d), F32)],
        compiler_params=_cp(("arbitrary",)),
        name="combine_ln2",
    )(yg, yg, yg, yg, tg, h, g, b)


def _lane_row(vals, offset, width=LANES):
    row = jnp.zeros((width,), F32)
    return lax.dynamic_update_slice(row, vals.astype(F32), (offset,)).reshape(1, width)


def _layer(x_all, P, past, nb, seq, nd, kv_prev):
    tp = nb * seq
    n_tok = tp + nd
    rb = tp // nd
    w_in = _prep_w_in(P['w_in'])
    hp_ = _matmul(x_all, w_in, 512, IN_TN)
    hs_ = hp_[tp:]

    hp = jnp.concatenate([_lane_row(P['ssd_dt_bias'], MISC_DT), _lane_row(P['ssd_a_log'], MISC_DT),
                          _lane_row(P['ssd_d'], MISC_DT), _lane_row(P['b_forget'], MISC_F),
                          jnp.zeros((4, LANES), F32)], axis=0)
    bf_row = hp[3:4]
    r1 = lambda a: a.reshape(1, -1).astype(F32)

    logf, c = _fox_prep(hp_, bf_row, nb, seq)
    a_out = _fox_prompt(hp_, c, nb, seq)
    b_out, ssm_p, sconv_p = _ssd_prompt(hp_, P['ssd_conv_w'], r1(P['ssd_conv_b']), hp, nb, seq)
    bst = jnp.pad(P['gmlp_bs'].T, ((0, 0), (0, LANES - GMLP_GROUPS)))
    c_out = _gmlp_prompt(hp_, r1(P['gmlp_ln_g']), r1(P['gmlp_ln_b']), P['gmlp_ws'], bst, tp)
    ccw = jnp.pad(P['conf_conv_w'], ((0, CONF_HALO - CONF_KERNEL), (0, 0)))
    d_out, conf_p = _conf_prompt(hp_, ccw, r1(P['conf_conv_b']), r1(P['conf_ln_g']), r1(P['conf_ln_b']), nb, seq)

    kv = _kv_state(hp_, tp, past['layer'], past['depth'], kv_prev)
    logf_p = logf[:, :FOX_HEADS].reshape(nb, seq, FOX_HEADS)
    state_p = (logf_p, ssm_p, sconv_p[:, SUBLANES - (SSD_CONV - 1):, :],
               conf_p[:, CONF_HALO - (CONF_KERNEL - 1):, :])

    w00 = jnp.repeat(P['gmlp_ws'][:, 0, 0], GROUP_W // GMLP_GROUPS).reshape(1, GROUP_W)
    b0 = jnp.repeat(P['gmlp_bs'][:, 0], GROUP_W // GMLP_GROUPS).reshape(1, GROUP_W)
    scw = jnp.pad(P['ssd_conv_w'], ((0, SUBLANES - SSD_CONV), (0, 0)))
    c_s, vrow_s, d_s, glu_s, xc_s, sm_s = _dec_rows(
        hp_, rb, nd, r1(P['gmlp_ln_g']), r1(P['gmlp_ln_b']), w00, b0,
        ccw, r1(P['conf_conv_b']), r1(P['conf_ln_g']), r1(P['conf_ln_b']), past['state_conv'],
        scw, r1(P['ssd_conv_b']), past['state_ssd_conv'], hp)
    yt, ssm_s = _dec_ssm(xc_s[:, :GROUP_W].T, hs_[:, COL_Z:COL_Z + GROUP_W].T, xc_s[:, GROUP_W:], sm_s, hp,
                         past['state_ssm'])
    b_s = yt.T
    a_s = _fox_decode(hp_, rb, sm_s, past['cache_k'], past['cache_v'], past['cache_lft'], past['page_table'],
                      past['layer'])
    k_s = hs_[:, COL_K:COL_K + GROUP_W].reshape(nd, 1, FOX_HEADS, FOX_HEAD_DIM)
    v_s = hs_[:, COL_V:COL_V + GROUP_W].reshape(nd, 1, FOX_HEADS, FOX_HEAD_DIM)
    logf_s = sm_s[:, :FOX_HEADS].reshape(nd, 1, FOX_HEADS)
    sconv_s = jnp.concatenate([past['state_ssd_conv'][:, 1:], hs_[:, None, COL_XBC:COL_XBC + SSD_CONV_DIM]], axis=1)
    conf_s = jnp.concatenate([past['state_conv'][:, 1:], glu_s[:, None, :]], axis=1)
    state_s = (k_s, v_s, logf_s, ssm_s, sconv_s, conf_s, vrow_s.reshape(nd, 1, GROUP_W))

    w_out = P['w_out'].astype(BF16)
    rw = jnp.pad(P['router_w'], ((0, 0), (0, LANES - N_EXPERTS)))
    rb = jnp.concatenate([P['router_b'].astype(F32), jnp.full((LANES - N_EXPERTS,), NEG, F32)]).reshape(1, LANES)
    op_args = (w_out, r1(P['mix_norm_g']), r1(P['ln1_g']), r1(P['ln1_b']), rw, rb)
    outs = _outproj(a_out, b_out, c_out, d_out, x_all, *op_args, tm=256, row0=0)
    h_all, hpk_all, ti_all, tg_all = _outproj(a_s, b_s, c_s, d_s, x_all, *op_args, tm=nd, row0=tp, prev=outs)

    row_tok, dest, ge, gb, gr, total = _moe_plan(ti_all[:, :TOP_K], n_tok)
    x_rows = _gather_rows(hpk_all, row_tok.reshape(-1), total.reshape(1), MOE_R)
    y_rows = _moe_ffn(x_rows, past['exp_w_gu'], past['exp_b_gu'], past['exp_w_dn'], past['exp_b_dn'], past['layer'],
                      ge, gb, gr)
    n_pairs = n_tok * TOP_K
    steps = max(s for s in range(1, 33) if (n_pairs // SUBLANES) % s == 0)
    yg = _gather_rows(y_rows, dest.T.reshape(-1), jnp.full((1,), steps, jnp.int32), n_pairs // steps)
    o_all = _combine(yg.reshape(TOP_K, n_tok, ROW_TILES_Y, LANES), tg_all, h_all, r1(P['ln2_g']), r1(P['ln2_b']),
                     tm=256)
    return o_all, state_p, state_s, kv


PARAM_NAMES = ('w_in', 'b_forget', 'ssd_conv_w', 'ssd_conv_b', 'ssd_dt_bias', 'ssd_a_log', 'ssd_d', 'gmlp_ln_g',
               'gmlp_ln_b', 'gmlp_ws', 'gmlp_bs', 'conf_conv_w', 'conf_conv_b', 'conf_ln_g', 'conf_ln_b',
               'mix_norm_g', 'w_out', 'ln1_g', 'ln1_b', 'router_w', 'router_b', 'exp_w_gu', 'exp_b_gu',
               'exp_w_dn', 'exp_b_dn', 'ln2_g', 'ln2_b')


def kernel(x_prompt, x_sample, cache_k, cache_v, cache_logf, state_ssm, state_ssd_conv, state_conv, page_table, w_in, b_forget, ssd_conv_w, ssd_conv_b, ssd_dt_bias, ssd_a_log, ssd_d, gmlp_ln_g, gmlp_ln_b, gmlp_ws, gmlp_bs, conf_conv_w, conf_conv_b, conf_ln_g, conf_ln_b, mix_norm_g, w_out, ln1_g, ln1_b, router_w, router_b, exp_w_gu, exp_b_gu, exp_w_dn, exp_b_dn, ln2_g, ln2_b):
    weights = (w_in, b_forget, ssd_conv_w, ssd_conv_b, ssd_dt_bias, ssd_a_log, ssd_d, gmlp_ln_g, gmlp_ln_b, gmlp_ws,
               gmlp_bs, conf_conv_w, conf_conv_b, conf_ln_g, conf_ln_b, mix_norm_g, w_out, ln1_g, ln1_b, router_w,
               router_b, exp_w_gu, exp_b_gu, exp_w_dn, exp_b_dn, ln2_g, ln2_b)
    nb, seq, d = x_prompt.shape
    nd = x_sample.shape[0]
    depth = w_in.shape[0]
    n_pool = cache_k.shape[1]
    ck = cache_k.reshape(depth, n_pool, PAGE_SIZE, GROUP_W)
    cv = cache_v.reshape(depth, n_pool, PAGE_SIZE, GROUP_W)
    clft = jnp.swapaxes(cache_logf, 2, 3)
    tp = nb * seq
    x_all = jnp.concatenate([x_prompt.reshape(tp, d), x_sample.reshape(nd, d)], axis=0)
    st_p, st_s = [], []
    kv = None
    big = ('exp_w_gu', 'exp_b_gu', 'exp_w_dn', 'exp_b_dn')
    for l in range(depth):
        P = {n: w[l] for n, w in zip(PARAM_NAMES, weights) if n not in big}
        past = dict(cache_k=ck, cache_v=cv, cache_lft=clft, layer=l, depth=depth, state_ssm=state_ssm[l],
                    state_ssd_conv=state_ssd_conv[l], state_conv=state_conv[l], page_table=page_table,
                    exp_w_gu=exp_w_gu, exp_b_gu=exp_b_gu, exp_w_dn=exp_w_dn, exp_b_dn=exp_b_dn)
        x_all, sp, ss, kv = _layer(x_all, P, past, nb, seq, nd, kv)
        st_p.append(sp)
        st_s.append(ss)
    stack = lambda sts, i: jnp.stack([s[i] for s in sts], axis=0)
    kv_shape = (depth, nb, seq, FOX_HEADS, FOX_HEAD_DIM)
    return (x_all[:tp].reshape(nb, seq, d), x_all[tp:].reshape(nd, 1, d),
            kv[0].reshape(kv_shape), kv[1].reshape(kv_shape), stack(st_p, 0),
            stack(st_s, 0), stack(st_s, 1), stack(st_s, 2),
            stack(st_p, 1), stack(st_s, 3), stack(st_p, 2), stack(st_s, 4),
            stack(st_p, 3), stack(st_s, 5), stack(st_s, 6))
```

```python
import functools
import math

import jax
import jax.numpy as jnp
from jax import lax
from jax.experimental import pallas as pl
from jax.experimental.pallas import tpu as pltpu

F32 = jnp.float32
BF16 = jnp.bfloat16

D_MODEL = 2048
GROUP_W = 512
N_MIXERS = 4
FOX_HEADS = 8
FOX_HEAD_DIM = 64
SSD_HEADS = 8
SSD_HEAD_DIM = 64
SSD_STATE = 128
SSD_GROUPS = 2
SSD_CONV = 4
SSD_CHUNK = 128
SSD_CONV_DIM = GROUP_W + 2 * SSD_GROUPS * SSD_STATE
GMLP_CHUNK = 128
GMLP_GROUPS = 4
CONF_KERNEL = 31
IN_SIZES = (GROUP_W, GROUP_W, GROUP_W, FOX_HEADS, GROUP_W, SSD_CONV_DIM, SSD_HEADS,
            GROUP_W, GROUP_W, GROUP_W, GROUP_W)
IN_SPLITS = tuple(sum(IN_SIZES[:i + 1]) for i in range(len(IN_SIZES) - 1))
N_EXPERTS = 32
TOP_K = 4
D_FF = D_MODEL
SWIGLU_LIMIT = 7.0
SWIGLU_ALPHA = 1.702
DEPTH = 2
DEEPNORM_ALPHA = (2 * DEPTH) ** 0.25
EPS = 1e-5
PAGE_SIZE = 128

LANES = 128
SUBLANES = 8
VMEM_LIMIT = 56 * 1024 * 1024

COL_Q, COL_K, COL_V, COL_Z, COL_XBC, COL_GU, COL_GV, COL_GA, COL_GB, COL_MISC = (
    0, 512, 1024, 1536, 2048, 3072, 3584, 4096, 4608, 5120)
MISC_F = 0
MISC_DT = 8
IN_COLS_PAD = 5376
IN_TN = 1792
NEG = -1e30

MOE_R = 1152
MOE_SMALL = 384
MOE_TF = 256
ROW_TILES_X = D_MODEL // 2 // LANES
ROW_TILES_Y = D_MODEL // LANES


def _cp(sem, vmem=VMEM_LIMIT):
    return pltpu.CompilerParams(dimension_semantics=sem, vmem_limit_bytes=vmem)


def _split3(x):
    hi = x.astype(BF16)
    r1 = x - hi.astype(F32)
    mid = r1.astype(BF16)
    lo = (r1 - mid.astype(F32)).astype(BF16)
    return hi, mid, lo


def _dot01_left(t01, x):
    hi, mid, lo = _split3(x)
    d = lambda b: jnp.dot(t01, b, preferred_element_type=F32)
    return d(hi) + d(mid) + d(lo)


def _dot01_right(x, t01):
    hi, mid, lo = _split3(x)
    d = lambda a: jnp.dot(a, t01, preferred_element_type=F32)
    return d(hi) + d(mid) + d(lo)


def _log_sigmoid(x):
    return jnp.minimum(x, 0.0) - jnp.log(1.0 + jnp.exp(-jnp.abs(x)))


def _softplus(x):
    return jnp.maximum(x, 0.0) + jnp.log(1.0 + jnp.exp(-jnp.abs(x)))


def _sigmoid(x):
    return 1.0 / (1.0 + jnp.exp(-x))


def _silu(x):
    return x * _sigmoid(x)


def _gelu(x):
    return 0.5 * x * (1.0 + jnp.tanh(math.sqrt(2.0 / math.pi) * (x + 0.044715 * (x * x * x))))


def _layer_norm(x, g, b):
    mu = jnp.mean(x, axis=-1, keepdims=True)
    xc = x - mu
    var = jnp.mean(xc * xc, axis=-1, keepdims=True)
    return xc * lax.rsqrt(var + EPS) * g + b


def _mm_kernel(x_ref, w_ref, o_ref):
    o_ref[...] = jnp.dot(x_ref[...].astype(BF16), w_ref[...], preferred_element_type=F32)


def _matmul(x, w, tm, tn):
    m, k = x.shape
    n = w.shape[1]
    return pl.pallas_call(
        _mm_kernel,
        grid=(n // tn, pl.cdiv(m, tm)),
        in_specs=[pl.BlockSpec((tm, k), lambda j, i: (i, 0)),
                  pl.BlockSpec((k, tn), lambda j, i: (0, j))],
        out_specs=pl.BlockSpec((tm, tn), lambda j, i: (i, j)),
        out_shape=jax.ShapeDtypeStruct((m, n), F32),
        compiler_params=_cp(("arbitrary", "arbitrary")),
        name="in_proj",
    )(x, w)


def _prep_w_in(w):
    q, k, v, f, z, xbc, dt, gu, gv, ga, gb = jnp.split(w, IN_SPLITS, axis=-1)
    d = w.shape[0]
    misc = jnp.concatenate([f, dt, jnp.zeros((d, LANES - 16), w.dtype)], axis=-1)
    pad = jnp.zeros((d, IN_COLS_PAD - (COL_MISC + LANES)), w.dtype)
    return jnp.concatenate([q, k, v, z, xbc, gu, gv, ga, gb, misc, pad], axis=-1).astype(BF16)


FOX_CH = 256


def _fox_prep_kernel(misc_ref, bf_ref, lf_ref, c_ref, carry):
    ch = misc_ref.shape[0]

    @pl.when(pl.program_id(1) == 0)
    def _():
        carry[...] = jnp.zeros_like(carry)

    lf = _log_sigmoid(misc_ref[...] + bf_ref[...])
    row = lax.broadcasted_iota(jnp.int32, (ch, ch), 0)
    col = lax.broadcasted_iota(jnp.int32, (ch, ch), 1)
    tri = jnp.where(row >= col, 1.0, 0.0).astype(BF16)
    cs = _dot01_left(tri, lf) + carry[...]
    carry[...] = cs[ch - 1:ch, :]
    lf_ref[...] = lf
    c_ref[...] = cs


def _fox_prep(h, bf_row, nb, seq):
    ch = FOX_CH
    nch = seq // ch
    t = nb * seq
    return pl.pallas_call(
        _fox_prep_kernel,
        grid=(nb, nch),
        in_specs=[pl.BlockSpec((ch, LANES), lambda b, c: (b * nch + c, COL_MISC // LANES)),
                  pl.BlockSpec((1, LANES), lambda b, c: (0, 0))],
        out_specs=[pl.BlockSpec((ch, LANES), lambda b, c: (b * nch + c, 0)),
                   pl.BlockSpec((ch, LANES), lambda b, c: (b * nch + c, 0))],
        out_shape=[jax.ShapeDtypeStruct((t, LANES), F32),
                   jax.ShapeDtypeStruct((t, LANES), F32)],
        scratch_shapes=[pltpu.VMEM((1, LANES), F32)],
        compiler_params=_cp(("arbitrary", "arbitrary")),
        name="fox_prep",
    )(h, bf_row)


def _fox_bias_lanes(c_col, q_side):
    hi, mid, lo = (t.astype(F32) for t in _split3(c_col))
    terms = (hi, mid, lo, 1.0, 1.0, 1.0) if q_side else (1.0, 1.0, 1.0, -hi, -mid, -lo)
    lane = lax.broadcasted_iota(jnp.int32, (1, FOX_HEAD_DIM), 1)
    ext = jnp.zeros((c_col.shape[0], FOX_HEAD_DIM), F32)
    for i, t in enumerate(terms):
        ext = jnp.where(lane == i, t, ext)
    return ext


def _fox_kernel(q_ref, k_ref, v_ref, c_ref, o_ref, kaug, vt, sc):
    tq = q_ref.shape[0]
    nblk = k_ref.shape[0] // tq
    qi = pl.program_id(1)
    hd = FOX_HEAD_DIM

    @pl.when(qi == 0)
    def _():
        for j in range(nblk):
            rows = slice(j * tq, (j + 1) * tq)
            kblk = k_ref[rows, :]
            cblk = c_ref[rows, :]
            for h in range(FOX_HEADS):
                kaug[h, rows, :] = jnp.concatenate(
                    [kblk[:, h * hd:(h + 1) * hd], _fox_bias_lanes(cblk[:, h:h + 1], False)], axis=1).astype(BF16)
            vt[j] = v_ref[rows, :].T.astype(BF16)

    qblk = q_ref[...] * (hd ** -0.5)
    cq = c_ref[pl.ds(pl.multiple_of(qi * tq, tq), tq), :]
    qaug = [jnp.concatenate([qblk[:, h * hd:(h + 1) * hd], _fox_bias_lanes(cq[:, h:h + 1], True)], axis=1).astype(BF16)
            for h in range(FOX_HEADS)]
    key = lax.broadcasted_iota(jnp.int32, (tq, tq), 0)
    qry = lax.broadcasted_iota(jnp.int32, (tq, tq), 1)
    causal = key <= qry
    nt = (((1,), (1,)), ((), ()))

    def score(j, stats, masked):
        start = pl.multiple_of(j * tq, tq)
        new = []
        for h in range(FOX_HEADS):
            m, l = stats[h]
            st = lax.dot_general(kaug[h, pl.ds(start, tq), :], qaug[h], nt, preferred_element_type=F32)
            if masked:
                st = jnp.where(causal, st, NEG)
            sc[h, j] = st
            m_new = jnp.maximum(m, jnp.max(st, axis=0, keepdims=True))
            l = jnp.exp(m - m_new) * l + jnp.sum(jnp.exp(st - m_new), axis=0, keepdims=True)
            new.append((m_new, l))
        return tuple(new)

    stats = tuple((jnp.full((1, tq), NEG, F32), jnp.zeros((1, tq), F32)) for _ in range(FOX_HEADS))
    stats = lax.fori_loop(0, qi, lambda j, c: score(j, c, False), stats)
    stats = score(qi, stats, True)

    def apply(j, accs):
        new = []
        for h in range(FOX_HEADS):
            m, l = stats[h]
            p = jnp.exp(sc[h, j] - m) / l
            new.append(accs[h] + jnp.dot(vt[j, h * hd:(h + 1) * hd, :], p.astype(BF16), preferred_element_type=F32))
        return tuple(new)

    accs = lax.fori_loop(0, qi + 1, apply, tuple(jnp.zeros((hd, tq), F32) for _ in range(FOX_HEADS)))
    o_ref[...] = jnp.concatenate(accs, axis=0).T


def _fox_prompt(h, c, nb, seq):
    tq = FOX_CH
    nq = seq // tq
    t = nb * seq
    return pl.pallas_call(
        _fox_kernel,
        grid=(nb, nq),
        in_specs=[pl.BlockSpec((tq, GROUP_W), lambda b, i: (b * nq + i, COL_Q // GROUP_W)),
                  pl.BlockSpec((seq, GROUP_W), lambda b, i: (b, COL_K // GROUP_W)),
                  pl.BlockSpec((seq, GROUP_W), lambda b, i: (b, COL_V // GROUP_W)),
                  pl.BlockSpec((seq, LANES), lambda b, i: (b, 0))],
        out_specs=pl.BlockSpec((tq, GROUP_W), lambda b, i: (b * nq + i, 0)),
        out_shape=jax.ShapeDtypeStruct((t, GROUP_W), F32),
        scratch_shapes=[pltpu.VMEM((FOX_HEADS, seq, 2 * FOX_HEAD_DIM), BF16),
                        pltpu.VMEM((nq, GROUP_W, tq), BF16),
                        pltpu.VMEM((FOX_HEADS, nq, tq, tq), F32)],
        compiler_params=_cp(("arbitrary", "arbitrary")),
        name="fox_prompt",
    )(h, h, h, c)


def _kv_state_kernel(k_ref, v_ref, *rest):
    ko_ref, vo_ref = rest[-2:]
    k = k_ref[...]
    v = v_ref[...]
    for h in range(FOX_HEADS):
        ko_ref[:, h, :] = k[:, h * FOX_HEAD_DIM:(h + 1) * FOX_HEAD_DIM]
        vo_ref[:, h, :] = v[:, h * FOX_HEAD_DIM:(h + 1) * FOX_HEAD_DIM]


def _kv_state(h, t, layer, depth, prev=None):
    tm = 512
    prev = () if prev is None else tuple(prev)
    osp = pl.BlockSpec((None, tm, FOX_HEADS, FOX_HEAD_DIM), lambda i: (layer, i, 0, 0))
    osd = jax.ShapeDtypeStruct((depth, t, FOX_HEADS, FOX_HEAD_DIM), F32)
    return pl.pallas_call(
        _kv_state_kernel,
        grid=(t // tm,),
        in_specs=[pl.BlockSpec((tm, GROUP_W), lambda i: (i, COL_K // GROUP_W)),
                  pl.BlockSpec((tm, GROUP_W), lambda i: (i, COL_V // GROUP_W))]
                 + [pl.BlockSpec(memory_space=pl.ANY)] * len(prev),
        out_specs=[osp, osp],
        out_shape=[osd, osd],
        input_output_aliases={2 + i: i for i in range(len(prev))},
        compiler_params=_cp(("arbitrary",)),
        name="kv_state",
    )(h, h, *prev)


DEC_PPS = 8


def _fox_dec_kernel(pt_ref, q_ref, kn_ref, vn_ref, lfn_ref, *rest):
    del pt_ref
    pps = DEC_PPS
    k_refs = rest[:pps]
    v_refs = rest[pps:2 * pps]
    lf_refs = rest[2 * pps:3 * pps]
    o_ref = rest[3 * pps]
    m_s, l_s, acc_s, carry_s, qm_s, snew_s, lg_s = rest[3 * pps + 1:]
    b = pl.program_id(0)
    s = pl.program_id(1)
    ns = pl.num_programs(1) // 2
    hrow = lax.broadcasted_iota(jnp.int32, (FOX_HEADS, GROUP_W), 0)
    hcol = lax.broadcasted_iota(jnp.int32, (FOX_HEADS, GROUP_W), 1) // FOX_HEAD_DIM
    headmask = hrow == hcol
    nt = (((1,), (1,)), ((), ()))

    @pl.when(s == 0)
    def _():
        qrow = q_ref[pl.ds(b, 1), :] * (FOX_HEAD_DIM ** -0.5)
        qm = jnp.where(headmask, jnp.broadcast_to(qrow, (FOX_HEADS, GROUP_W)), 0.0)
        qm_s[...] = qm.astype(BF16)
        kn = kn_ref[pl.ds(b, 1), :].astype(BF16).astype(F32)
        snew = jnp.sum(qm.astype(BF16).astype(F32) * kn, axis=-1, keepdims=True)
        snew_s[...] = snew
        m_s[...] = snew
        lfn = jnp.broadcast_to(lfn_ref[pl.ds(b, 1), :], (FOX_HEADS, LANES))
        er = lax.broadcasted_iota(jnp.int32, (FOX_HEADS, LANES), 0)
        ec = lax.broadcasted_iota(jnp.int32, (FOX_HEADS, LANES), 1)
        carry_s[...] = jnp.sum(jnp.where(er == ec, lfn, 0.0), axis=-1, keepdims=True)

    @pl.when(s < ns)
    def _():
        r = lax.broadcasted_iota(jnp.int32, (PAGE_SIZE, PAGE_SIZE), 0)
        c = lax.broadcasted_iota(jnp.int32, (PAGE_SIZE, PAGE_SIZE), 1)
        strict = jnp.where(r > c, 1.0, 0.0).astype(BF16)
        qm = qm_s[...]
        m, carry = m_s[...], carry_s[...]
        for i in range(pps):
            kp = k_refs[i][...].astype(BF16)
            lf = lf_refs[i][...]
            st = lax.dot_general(qm, kp, nt, preferred_element_type=F32)
            logits = st + (_dot01_right(lf, strict) + carry)
            carry = carry + jnp.sum(lf, axis=-1, keepdims=True)
            m = jnp.maximum(m, jnp.max(logits, axis=-1, keepdims=True))
            lg_s[s * pps + i] = logits
        m_s[...], carry_s[...] = m, carry

    @pl.when(s == ns - 1)
    def _():
        m = m_s[...]

        def add(j, l):
            return l + jnp.sum(jnp.exp(lg_s[j] - m), axis=-1, keepdims=True)

        l_s[...] = lax.fori_loop(0, ns * pps, add, jnp.exp(snew_s[...] - m))
        acc_s[...] = jnp.zeros_like(acc_s)

    @pl.when(s >= ns)
    def _():
        m, l = m_s[...], l_s[...]
        acc = acc_s[...]
        for i in range(pps):
            vp = v_refs[i][...].astype(BF16)
            p = jnp.exp(lg_s[(s - ns) * pps + i] - m) / l
            acc = acc + jnp.dot(p.astype(BF16), vp, preferred_element_type=F32)
        acc_s[...] = acc

    @pl.when(s == 2 * ns - 1)
    def _():
        m, l = m_s[...], l_s[...]
        pn = (jnp.exp(snew_s[...] - m) / l).astype(BF16).astype(F32)
        vn = jnp.broadcast_to(vn_ref[pl.ds(b, 1), :].astype(BF16).astype(F32), (FOX_HEADS, GROUP_W))
        o_ref[...] = jnp.sum(jnp.where(headmask, acc_s[...] + pn * vn, 0.0), axis=0, keepdims=True)


def _fox_decode(hs, rb, lfn, ck, cv, clft, page_table, layer):
    nb, npg = page_table.shape
    pps = DEC_PPS
    ns = npg // pps
    pt = page_table.reshape(-1)

    def page_map(i, phase):
        def index(b, s, pt_ref):
            sp = jnp.minimum(s, ns - 1) if phase == 0 else jnp.maximum(s - ns, 0)
            return (layer, pt_ref[b * npg + (npg - 1 - (sp * pps + i))], 0, 0)
        return index

    full = lambda cb: pl.BlockSpec((nb, GROUP_W), lambda b, s, pt_ref: (rb, cb))
    in_specs = [full(COL_Q // GROUP_W), full(COL_K // GROUP_W), full(COL_V // GROUP_W),
                pl.BlockSpec((nb, LANES), lambda b, s, pt_ref: (0, 0))]
    in_specs += [pl.BlockSpec((None, None, PAGE_SIZE, GROUP_W), page_map(i, 0)) for i in range(pps)]
    in_specs += [pl.BlockSpec((None, None, PAGE_SIZE, GROUP_W), page_map(i, 1)) for i in range(pps)]
    in_specs += [pl.BlockSpec((None, None, FOX_HEADS, PAGE_SIZE), page_map(i, 0)) for i in range(pps)]
    out = pl.pallas_call(
        _fox_dec_kernel,
        grid_spec=pltpu.PrefetchScalarGridSpec(
            num_scalar_prefetch=1,
            grid=(nb, 2 * ns),
            in_specs=in_specs,
            out_specs=pl.BlockSpec((None, 1, GROUP_W), lambda b, s, pt_ref: (b, 0, 0)),
            scratch_shapes=[pltpu.VMEM((FOX_HEADS, 1), F32), pltpu.VMEM((FOX_HEADS, 1), F32),
                            pltpu.VMEM((FOX_HEADS, GROUP_W), F32), pltpu.VMEM((FOX_HEADS, 1), F32),
                            pltpu.VMEM((FOX_HEADS, GROUP_W), BF16), pltpu.VMEM((FOX_HEADS, 1), F32),
                            pltpu.VMEM((npg, FOX_HEADS, PAGE_SIZE), F32)]),
        out_shape=jax.ShapeDtypeStruct((nb, 1, GROUP_W), F32),
        compiler_params=_cp(("arbitrary", "arbitrary")),
        name="fox_decode",
    )(pt, hs, hs, hs, lfn, *([ck] * pps), *([cv] * pps), *([clft] * pps))
    return out.reshape(nb, GROUP_W)


def _ssd_kernel(xbc_ref, z_ref, misc_ref, cw_ref, cb_ref, hp_ref, o_ref, st_ref, cs_ref, carry, hstate):
    q = SSD_CHUNK
    c = pl.program_id(1)
    nc = pl.num_programs(1)

    @pl.when(c == 0)
    def _():
        carry[...] = jnp.zeros_like(carry)
        hstate[...] = jnp.zeros_like(hstate)

    xin = xbc_ref[...]
    full = jnp.concatenate([carry[...], xin], axis=0).astype(BF16).astype(F32)
    cw = cw_ref[...].astype(BF16).astype(F32)
    conv = cb_ref[...] + sum(full[SUBLANES - (SSD_CONV - 1) + w:SUBLANES - (SSD_CONV - 1) + w + q, :] * cw[w:w + 1, :]
                             for w in range(SSD_CONV))
    carry[...] = xin[q - SUBLANES:q, :]
    cs_ref[...] = xin[q - SUBLANES:q, :]
    xc = _silu(conv)
    xs = xc[:, :GROUP_W]

    dt = _softplus(misc_ref[...] + hp_ref[0:1, :])
    a_row = -jnp.exp(hp_ref[1:2, :])
    d_row = hp_ref[2:3, :]
    da = dt * a_row
    row = lax.broadcasted_iota(jnp.int32, (q, q), 0)
    col = lax.broadcasted_iota(jnp.int32, (q, q), 1)
    tril = row >= col
    tril01 = jnp.where(tril, 1.0, 0.0).astype(BF16)
    triu01 = jnp.where(row <= col, 1.0, 0.0).astype(BF16)
    acs = _dot01_left(tril01, da)
    dt_t = dt.T
    acs_t = _dot01_right(da.T, triu01)
    nt = (((1,), (1,)), ((), ()))
    tn = (((0,), (0,)), ((), ()))
    rep = SSD_HEADS // SSD_GROUPS
    for g in range(SSD_GROUPS):
        bm = xc[:, GROUP_W + g * SSD_STATE:GROUP_W + (g + 1) * SSD_STATE].astype(BF16)
        cm = xc[:, GROUP_W + (SSD_GROUPS + g) * SSD_STATE:GROUP_W + (SSD_GROUPS + g + 1) * SSD_STATE].astype(BF16)
        cbm = lax.dot_general(cm, bm, nt, preferred_element_type=F32)
        for r in range(rep):
            hd = g * rep + r
            ln = MISC_DT + hd
            lo, hi = hd * SSD_HEAD_DIM, (hd + 1) * SSD_HEAD_DIM
            a_col = acs[:, ln:ln + 1]
            a_rw = acs_t[ln:ln + 1, :]
            dt_rw = dt_t[ln:ln + 1, :]
            dt_col = dt[:, ln:ln + 1]
            a_last = acs[q - 1:q, ln:ln + 1]
            decay = jnp.where(tril, jnp.exp(jnp.minimum(a_col - a_rw, 0.0)), 0.0)
            mm = (cbm * decay * dt_rw).astype(BF16)
            xh = xs[:, lo:hi]
            y = jnp.dot(mm, xh.astype(BF16), preferred_element_type=F32)
            xw = (xh * (jnp.exp(a_last - a_col) * dt_col)).astype(BF16)
            states = lax.dot_general(xw, bm, tn, preferred_element_type=F32)
            hprev = hstate[hd]
            y_off = lax.dot_general(cm, hprev.astype(BF16), nt, preferred_element_type=F32)
            y = y + y_off * jnp.exp(a_col) + d_row[:, ln:ln + 1] * xh
            hstate[hd] = hprev * jnp.exp(a_last) + states
            o_ref[:, lo:hi] = y
    o_ref[...] = o_ref[...] * _silu(z_ref[...])

    @pl.when(c == nc - 1)
    def _():
        st_ref[...] = hstate[...]


def _ssd_prompt(h, cw, cb, hp, nb, seq):
    q = SSD_CHUNK
    nc = seq // q
    t = nb * seq
    return pl.pallas_call(
        _ssd_kernel,
        grid=(nb, nc),
        in_specs=[pl.BlockSpec((q, SSD_CONV_DIM), lambda b, c: (b * nc + c, COL_XBC // SSD_CONV_DIM)),
                  pl.BlockSpec((q, GROUP_W), lambda b, c: (b * nc + c, COL_Z // GROUP_W)),
                  pl.BlockSpec((q, LANES), lambda b, c: (b * nc + c, COL_MISC // LANES)),
                  pl.BlockSpec((SSD_CONV, SSD_CONV_DIM), lambda b, c: (0, 0)),
                  pl.BlockSpec((1, SSD_CONV_DIM), lambda b, c: (0, 0)),
                  pl.BlockSpec((SUBLANES, LANES), lambda b, c: (0, 0))],
        out_specs=[pl.BlockSpec((q, GROUP_W), lambda b, c: (b * nc + c, 0)),
                   pl.BlockSpec((None, SSD_HEADS, SSD_HEAD_DIM, SSD_STATE), lambda b, c: (b, 0, 0, 0)),
                   pl.BlockSpec((None, SUBLANES, SSD_CONV_DIM), lambda b, c: (b, 0, 0))],
        out_shape=[jax.ShapeDtypeStruct((t, GROUP_W), F32),
                   jax.ShapeDtypeStruct((nb, SSD_HEADS, SSD_HEAD_DIM, SSD_STATE), F32),
                   jax.ShapeDtypeStruct((nb, SUBLANES, SSD_CONV_DIM), F32)],
        scratch_shapes=[pltpu.VMEM((SUBLANES, SSD_CONV_DIM), F32),
                        pltpu.VMEM((SSD_HEADS, SSD_HEAD_DIM, SSD_STATE), F32)],
        compiler_params=_cp(("arbitrary", "arbitrary")),
        name="ssd_prompt",
    )(h, h, h, cw, cb, hp)


def _gmlp_kernel(u_ref, v_ref, lng_ref, lnb_ref, ws_ref, bst_ref, o_ref):
    n = GMLP_CHUNK
    v = _layer_norm(_gelu(v_ref[...]), lng_ref[...], lnb_ref[...])
    row = lax.broadcasted_iota(jnp.int32, (n, n), 0)
    col = lax.broadcasted_iota(jnp.int32, (n, n), 1)
    tril = row >= col
    gc = GROUP_W // GMLP_GROUPS
    parts = []
    for g in range(GMLP_GROUPS):
        wm = jnp.where(tril, ws_ref[g], 0.0).astype(BF16)
        sg = jnp.dot(wm, v[:, g * gc:(g + 1) * gc].astype(BF16), preferred_element_type=F32)
        parts.append(sg + bst_ref[:, g:g + 1])
    s = jnp.concatenate(parts, axis=-1)
    o_ref[...] = _gelu(u_ref[...]) * s


def _gmlp_prompt(h, lng, lnb, ws, bst, t):
    n = GMLP_CHUNK
    return pl.pallas_call(
        _gmlp_kernel,
        grid=(t // n,),
        in_specs=[pl.BlockSpec((n, GROUP_W), lambda i: (i, COL_GU // GROUP_W)),
                  pl.BlockSpec((n, GROUP_W), lambda i: (i, COL_GV // GROUP_W)),
                  pl.BlockSpec((1, GROUP_W), lambda i: (0, 0)),
                  pl.BlockSpec((1, GROUP_W), lambda i: (0, 0)),
                  pl.BlockSpec((GMLP_GROUPS, n, n), lambda i: (0, 0, 0)),
                  pl.BlockSpec((n, LANES), lambda i: (0, 0))],
        out_specs=pl.BlockSpec((n, GROUP_W), lambda i: (i, 0)),
        out_shape=jax.ShapeDtypeStruct((t, GROUP_W), F32),
        compiler_params=_cp(("arbitrary",)),
        name="gmlp_prompt",
    )(h, h, lng, lnb, ws, bst)


CONF_HALO = 32
CONF_CHUNK = 128


def _conf_kernel(a_ref, g_ref, cw_ref, cb_ref, lng_ref, lnb_ref, o_ref, st_ref, carry):
    n = CONF_CHUNK

    @pl.when(pl.program_id(1) == 0)
    def _():
        carry[...] = jnp.zeros_like(carry)

    glu = a_ref[...] * _sigmoid(g_ref[...])
    full = jnp.concatenate([carry[...], glu], axis=0).astype(BF16).astype(F32)
    cw = cw_ref[...].astype(BF16).astype(F32)
    off = CONF_HALO - (CONF_KERNEL - 1)
    acc = cb_ref[...] + full[off:off + n, :] * cw[0:1, :]
    for w in range(1, CONF_KERNEL):
        acc = acc + full[off + w:off + w + n, :] * cw[w:w + 1, :]
    carry[...] = glu[n - CONF_HALO:n, :]
    st_ref[...] = glu[n - CONF_HALO:n, :]
    o_ref[...] = _silu(_layer_norm(acc, lng_ref[...], lnb_ref[...]))


def _conf_prompt(h, cw, cb, lng, lnb, nb, seq):
    n = CONF_CHUNK
    nc = seq // n
    t = nb * seq
    return pl.pallas_call(
        _conf_kernel,
        grid=(nb, nc),
        in_specs=[pl.BlockSpec((n, GROUP_W), lambda b, c: (b * nc + c, COL_GA // GROUP_W)),
                  pl.BlockSpec((n, GROUP_W), lambda b, c: (b * nc + c, COL_GB // GROUP_W)),
                  pl.BlockSpec((CONF_HALO, GROUP_W), lambda b, c: (0, 0)),
                  pl.BlockSpec((1, GROUP_W), lambda b, c: (0, 0)),
                  pl.BlockSpec((1, GROUP_W), lambda b, c: (0, 0)),
                  pl.BlockSpec((1, GROUP_W), lambda b, c: (0, 0))],
        out_specs=[pl.BlockSpec((n, GROUP_W), lambda b, c: (b * nc + c, 0)),
                   pl.BlockSpec((None, CONF_HALO, GROUP_W), lambda b, c: (b, 0, 0))],
        out_shape=[jax.ShapeDtypeStruct((t, GROUP_W), F32),
                   jax.ShapeDtypeStruct((nb, CONF_HALO, GROUP_W), F32)],
        scratch_shapes=[pltpu.VMEM((CONF_HALO, GROUP_W), F32)],
        compiler_params=_cp(("arbitrary", "arbitrary")),
        name="conf_prompt",
    )(h, h, cw, cb, lng, lnb)


def _dec_rows_kernel(gu_ref, gv_ref, ga_ref, gb_ref, xbc_ref, misc_ref,
                     lng_ref, lnb_ref, w00_ref, b0_ref,
                     ccw_ref, ccb_ref, clg_ref, clb_ref, cprev_ref,
                     scw_ref, scb_ref, sprev_ref, hp_ref,
                     c_ref, vrow_ref, d_ref, glu_ref, xc_ref, sm_ref):
    nb = gu_ref.shape[0]
    v = _layer_norm(_gelu(gv_ref[...]), lng_ref[...], lnb_ref[...])
    vrow_ref[...] = v
    c_ref[...] = _gelu(gu_ref[...]) * (w00_ref[...] * v + b0_ref[...])
    glu = ga_ref[...] * _sigmoid(gb_ref[...])
    glu_ref[...] = glu
    kw = CONF_KERNEL - 1
    rows = []
    for b in range(nb):
        rows.append(jnp.sum(cprev_ref[b] * ccw_ref[0:kw, :], axis=0, keepdims=True))
    conv = jnp.concatenate(rows, axis=0) + glu * ccw_ref[kw:kw + 1, :] + ccb_ref[...]
    d_ref[...] = _silu(_layer_norm(conv, clg_ref[...], clb_ref[...]))
    sk = SSD_CONV - 1
    rows = []
    for b in range(nb):
        rows.append(jnp.sum(sprev_ref[b] * scw_ref[0:sk, :], axis=0, keepdims=True))
    sconv = jnp.concatenate(rows, axis=0) + xbc_ref[...] * scw_ref[sk:sk + 1, :] + scb_ref[...]
    xc_ref[...] = _silu(sconv)
    misc = misc_ref[...]
    lane = lax.broadcasted_iota(jnp.int32, misc.shape, 1)
    lf = _log_sigmoid(misc + hp_ref[3:4, :])
    dt = _softplus(misc + hp_ref[0:1, :])
    sm_ref[...] = jnp.where(lane < MISC_DT, lf, dt)


def _dec_rows(hs, rb, nb, lng, lnb, w00, b0, ccw, ccb, clg, clb, cprev, scw, scb, sprev, hp):
    col = lambda c0, w: pl.BlockSpec((nb, w), lambda i: (rb, c0 // w))
    whole = lambda a: pl.BlockSpec(a.shape, lambda i: (0,) * a.ndim)
    sd = lambda w: jax.ShapeDtypeStruct((nb, w), F32)
    return pl.pallas_call(
        _dec_rows_kernel,
        grid=(1,),
        in_specs=[col(COL_GU, GROUP_W), col(COL_GV, GROUP_W), col(COL_GA, GROUP_W), col(COL_GB, GROUP_W),
                  col(COL_XBC, SSD_CONV_DIM), col(COL_MISC, LANES)]
                 + [whole(a) for a in (lng, lnb, w00, b0, ccw, ccb, clg, clb, cprev, scw, scb, sprev, hp)],
        out_specs=[pl.BlockSpec((nb, GROUP_W), lambda i: (0, 0))] * 4
                  + [pl.BlockSpec((nb, SSD_CONV_DIM), lambda i: (0, 0)), pl.BlockSpec((nb, LANES), lambda i: (0, 0))],
        out_shape=[sd(GROUP_W)] * 4 + [sd(SSD_CONV_DIM), sd(LANES)],
        compiler_params=_cp(("arbitrary",)),
        name="dec_rows",
    )(hs, hs, hs, hs, hs, hs, lng, lnb, w00, b0, ccw, ccb, clg, clb, cprev, scw, scb, sprev, hp)


def _dec_ssm_kernel(xt_ref, zt_ref, bc_ref, sm_ref, hp_ref, h_ref, yt_ref, hn_ref):
    nb = bc_ref.shape[0]
    rep = SSD_HEADS // SSD_GROUPS
    a_row = -jnp.exp(hp_ref[1:2, :])
    d_row = hp_ref[2:3, :]
    sm = sm_ref[...]
    for b in range(nb):
        cols = []
        for hd in range(SSD_HEADS):
            g = hd // rep
            ln = MISC_DT + hd
            lo, hi = hd * SSD_HEAD_DIM, (hd + 1) * SSD_HEAD_DIM
            dt = sm[b:b + 1, ln:ln + 1]
            x = xt_ref[lo:hi, b:b + 1]
            brow = bc_ref[b:b + 1, g * SSD_STATE:(g + 1) * SSD_STATE]
            crow = bc_ref[b:b + 1, (SSD_GROUPS + g) * SSD_STATE:(SSD_GROUPS + g + 1) * SSD_STATE]
            hnew = h_ref[b, hd] * jnp.exp(dt * a_row[:, ln:ln + 1]) + (dt * x) * brow
            hn_ref[b, hd] = hnew
            y = jnp.sum(hnew.astype(BF16).astype(F32) * crow.astype(BF16).astype(F32), axis=-1, keepdims=True)
            cols.append(y + d_row[:, ln:ln + 1] * x)
        ycol = jnp.concatenate(cols, axis=0)
        yt_ref[:, b:b + 1] = ycol * _silu(zt_ref[:, b:b + 1])


def _dec_ssm(xt, zt, bc, sm, hp, hstate):
    nb = bc.shape[0]
    whole = lambda a: pl.BlockSpec(a.shape, lambda i: (0,) * a.ndim)
    return pl.pallas_call(
        _dec_ssm_kernel,
        grid=(1,),
        in_specs=[whole(a) for a in (xt, zt, bc, sm, hp, hstate)],
        out_specs=[pl.BlockSpec((GROUP_W, nb), lambda i: (0, 0)),
                   pl.BlockSpec(hstate.shape, lambda i: (0, 0, 0, 0))],
        out_shape=[jax.ShapeDtypeStruct((GROUP_W, nb), F32), jax.ShapeDtypeStruct(hstate.shape, F32)],
        compiler_params=_cp(("arbitrary",)),
        name="dec_ssm",
    )(xt, zt, bc, sm, hp, hstate)


def _rows_to_tiles(pieces):
    return jnp.swapaxes(jnp.stack(pieces, axis=0), 0, 1)


def _tiles_to_rows(block):
    t = jnp.swapaxes(block, 0, 1)
    return [t[s] for s in range(SUBLANES)]


def _pack_bf16_pair(lo, hi):
    lo_bits = lax.bitcast_convert_type(lo.astype(BF16).astype(F32), jnp.uint32)
    hi_bits = lax.bitcast_convert_type(hi.astype(BF16).astype(F32), jnp.uint32)
    return (lo_bits >> 16) | (hi_bits & jnp.uint32(0xFFFF0000))


def _unpack_bf16_pair(w):
    lo = lax.bitcast_convert_type(w << 16, F32).astype(BF16)
    hi = lax.bitcast_convert_type(w & jnp.uint32(0xFFFF0000), F32).astype(BF16)
    return lo, hi


def _outproj_kernel(a_ref, b_ref, c_ref, d_ref, x_ref, w_ref, mg_ref, g_ref, bb_ref, rw_ref, rb_ref, *rest):
    h_ref, hb_ref, ti_ref, tg_ref = rest[-4:]
    y = None
    for i, m_ref in enumerate((a_ref, b_ref, c_ref, d_ref)):
        m = m_ref[...]
        r = lax.rsqrt(jnp.mean(m * m, axis=-1, keepdims=True) + EPS)
        nrm = (m * r * mg_ref[:, i * GROUP_W:(i + 1) * GROUP_W]).astype(BF16)
        part = jnp.dot(nrm, w_ref[i * GROUP_W:(i + 1) * GROUP_W, :], preferred_element_type=F32)
        y = part if y is None else y + part
    h = _layer_norm(DEEPNORM_ALPHA * x_ref[...] + y, g_ref[...], bb_ref[...])
    h_ref[...] = h
    packed = _pack_bf16_pair(h[:, :D_MODEL // 2], h[:, D_MODEL // 2:])
    for i in range(h.shape[0] // SUBLANES):
        rows = slice(i * SUBLANES, (i + 1) * SUBLANES)
        hb_ref[rows] = _rows_to_tiles([packed[rows, s * LANES:(s + 1) * LANES] for s in range(ROW_TILES_X)])
    lg = jnp.dot(h.astype(BF16), rw_ref[...].astype(BF16), preferred_element_type=F32) + rb_ref[...]
    lane = lax.broadcasted_iota(jnp.int32, lg.shape, 1)
    ti = jnp.zeros(lg.shape, jnp.int32)
    tv = jnp.full(lg.shape, NEG, F32)
    for k in range(TOP_K):
        mx = jnp.max(lg, axis=-1, keepdims=True)
        idx = jnp.min(jnp.where(lg == mx, lane, LANES), axis=-1, keepdims=True)
        ti = jnp.where(lane == k, idx, ti)
        tv = jnp.where(lane == k, mx, tv)
        lg = jnp.where(lane == idx, NEG, lg)
    e = jnp.exp(tv - jnp.max(tv, axis=-1, keepdims=True))
    tg_ref[...] = e / jnp.sum(e, axis=-1, keepdims=True)
    ti_ref[...] = ti


def _outproj(a, b, c, d, x_all, w, mg, g, bb, rw, rb, tm, row0, prev=None):
    t_all = x_all.shape[0]
    n = a.shape[0]
    off = row0 // tm
    row = lambda w_: pl.BlockSpec((tm, w_), lambda i: (i, 0))
    row_off = lambda w_: pl.BlockSpec((tm, w_), lambda i: (i + off, 0))
    whole = lambda arr: pl.BlockSpec(arr.shape, lambda i: (0,) * arr.ndim)
    prev = () if prev is None else tuple(prev)
    n_in = 11
    return pl.pallas_call(
        _outproj_kernel,
        grid=(n // tm,),
        in_specs=[row(GROUP_W)] * 4 + [row_off(D_MODEL)] + [whole(arr) for arr in (w, mg, g, bb, rw, rb)]
                 + [pl.BlockSpec(memory_space=pl.ANY)] * len(prev),
        out_specs=[row_off(D_MODEL), pl.BlockSpec((tm, ROW_TILES_X, LANES), lambda i: (i + off, 0, 0)),
                   row_off(LANES), row_off(LANES)],
        out_shape=[jax.ShapeDtypeStruct((t_all, D_MODEL), F32),
                   jax.ShapeDtypeStruct((t_all, ROW_TILES_X, LANES), jnp.uint32),
                   jax.ShapeDtypeStruct((t_all, LANES), jnp.int32), jax.ShapeDtypeStruct((t_all, LANES), F32)],
        input_output_aliases={n_in + i: i for i in range(len(prev))},
        compiler_params=_cp(("arbitrary",)),
        name="outproj_ln_router",
    )(a, b, c, d, x_all, w, mg, g, bb, rw, rb, *prev)


def _moe_kernel(ge_ref, gb_ref, gr_ref, x_ref, wg_ref, wu_ref, wd_ref, bg_ref, bu_ref, bd_ref, y_ref,
                xb, yacc, wgu, wdb):
    del ge_ref, gb_ref
    g = pl.program_id(0)
    f = pl.program_id(1)
    nf = pl.num_programs(1)
    rows = gr_ref[g]
    half = D_MODEL // 2

    @pl.when(rows > 0)
    def _():
        @pl.when(f == 0)
        def _():
            def unpack16(i, carry):
                r0 = pl.multiple_of(i * 2 * SUBLANES, 2 * SUBLANES)
                top = _tiles_to_rows(x_ref[pl.ds(r0, SUBLANES)])
                bot = _tiles_to_rows(x_ref[pl.ds(r0 + SUBLANES, SUBLANES)])
                for s in range(ROW_TILES_X):
                    lo, hi = _unpack_bf16_pair(jnp.concatenate([top[s], bot[s]], axis=0))
                    xb[pl.ds(r0, 2 * SUBLANES), s * LANES:(s + 1) * LANES] = lo
                    xb[pl.ds(r0, 2 * SUBLANES), half + s * LANES:half + (s + 1) * LANES] = hi
                return carry

            lax.fori_loop(0, MOE_R // (2 * SUBLANES), unpack16, 0, unroll=2)
            yacc[...] = jnp.broadcast_to(bd_ref[...], yacc.shape)

        wgu[:, :MOE_TF] = wg_ref[...].astype(BF16)
        wgu[:, MOE_TF:] = wu_ref[...].astype(BF16)
        wdb[...] = wd_ref[...].astype(BF16)

        def ffn(nrows):
            hgu = jnp.dot(xb[0:nrows, :], wgu[...], preferred_element_type=F32)
            gate = jnp.minimum(hgu[:, :MOE_TF] + bg_ref[...], SWIGLU_LIMIT)
            up = jnp.clip(hgu[:, MOE_TF:] + bu_ref[...], -SWIGLU_LIMIT, SWIGLU_LIMIT)
            act = (up + 1.0) * (gate * _sigmoid(SWIGLU_ALPHA * gate))
            yacc[0:nrows, :] += jnp.dot(act.astype(BF16), wdb[...], preferred_element_type=F32)

        @pl.when(rows > MOE_SMALL)
        def _():
            ffn(MOE_R)

        @pl.when(rows <= MOE_SMALL)
        def _():
            ffn(MOE_SMALL)

        @pl.when(f == nf - 1)
        def _():
            def to_tiles(i, carry):
                r0 = pl.multiple_of(i * SUBLANES, SUBLANES)
                for t0 in range(0, ROW_TILES_Y, SUBLANES):
                    y_ref[pl.ds(r0, SUBLANES), t0:t0 + SUBLANES, :] = _rows_to_tiles(
                        [yacc[pl.ds(r0, SUBLANES), (t0 + s) * LANES:(t0 + s + 1) * LANES] for s in range(SUBLANES)])
                return carry

            lax.fori_loop(0, MOE_R // SUBLANES, to_tiles, 0, unroll=2)


def _moe_ffn(x_rows, w_gu, b_gu, w_dn, b_dn, layer, ge, gb, gr):
    ng = ge.shape[0]
    d = D_MODEL
    nf = D_FF // MOE_TF
    r = MOE_R
    depth = w_gu.shape[0]

    def fidx(g, f, gr_ref):
        return jnp.where(gr_ref[g] > 0, f, nf - 1)

    in_specs = [
        pl.BlockSpec((r, ROW_TILES_X, LANES), lambda g, f, ge_, gb_, gr_: (gb_[g], 0, 0)),
        pl.BlockSpec((None, None, d, MOE_TF), lambda g, f, ge_, gb_, gr_: (layer, ge_[g], 0, fidx(g, f, gr_))),
        pl.BlockSpec((None, None, d, MOE_TF), lambda g, f, ge_, gb_, gr_: (layer, ge_[g], 0, nf + fidx(g, f, gr_))),
        pl.BlockSpec((None, None, MOE_TF, d), lambda g, f, ge_, gb_, gr_: (layer, ge_[g], fidx(g, f, gr_), 0)),
        pl.BlockSpec((None, None, 1, MOE_TF), lambda g, f, ge_, gb_, gr_: (layer, ge_[g], 0, fidx(g, f, gr_))),
        pl.BlockSpec((None, None, 1, MOE_TF), lambda g, f, ge_, gb_, gr_: (layer, ge_[g], 0, nf + fidx(g, f, gr_))),
        pl.BlockSpec((None, None, 1, d), lambda g, f, ge_, gb_, gr_: (layer, ge_[g], 0, 0)),
    ]
    b_gu4 = b_gu.reshape(depth, N_EXPERTS, 1, -1)
    return pl.pallas_call(
        _moe_kernel,
        grid_spec=pltpu.PrefetchScalarGridSpec(
            num_scalar_prefetch=3,
            grid=(ng, nf),
            in_specs=in_specs,
            out_specs=pl.BlockSpec((r, ROW_TILES_Y, LANES), lambda g, f, ge_, gb_, gr_: (gb_[g], 0, 0),
                                   pipeline_mode=pl.Buffered(1)),
            scratch_shapes=[pltpu.VMEM((r, d), BF16), pltpu.VMEM((r, d), F32), pltpu.VMEM((d, 2 * MOE_TF), BF16),
                            pltpu.VMEM((MOE_TF, d), BF16)]),
        out_shape=jax.ShapeDtypeStruct((ng * r, ROW_TILES_Y, LANES), F32),
        compiler_params=_cp(("arbitrary", "arbitrary")),
        name="moe_ffn",
    )(ge, gb, gr, x_rows, w_gu, w_gu, w_dn, b_gu4, b_gu4, b_dn.reshape(depth, N_EXPERTS, 1, -1))


def _moe_plan(top_i, n_tok):
    n_pairs = n_tok * TOP_K
    r = MOE_R
    ng = n_pairs // r + N_EXPERTS
    flat_e = top_i.reshape(-1)
    onehot = (flat_e[:, None] == jnp.arange(N_EXPERTS, dtype=jnp.int32)[None, :]).astype(jnp.int32)
    csum = jnp.cumsum(onehot, axis=0)
    counts = csum[-1]
    rank = jnp.sum(onehot * csum, axis=1) - 1
    ngrp = (counts + r - 1) // r
    grp_end = jnp.cumsum(ngrp)
    grp_start = grp_end - ngrp
    dest = jnp.sum(onehot * grp_start[None, :], axis=1) * r + rank
    pair_tok = jnp.arange(n_pairs, dtype=jnp.int32) // TOP_K
    row_tok = jnp.zeros((ng * r,), jnp.int32).at[dest].set(pair_tok, unique_indices=True)
    total = grp_end[-1]
    gidx = jnp.arange(ng, dtype=jnp.int32)
    valid = gidx < total
    gclamp = jnp.minimum(gidx, total - 1)
    ge = jnp.sum(grp_end[None, :] <= gclamp[:, None], axis=1).astype(jnp.int32)
    local = gclamp - grp_start[ge]
    gr = jnp.where(valid, jnp.clip(counts[ge] - local * r, 0, r), 0).astype(jnp.int32)
    return row_tok, dest.reshape(n_tok, TOP_K).astype(jnp.int32), ge, gclamp.astype(jnp.int32), gr, total


GATHER_UNROLL = 8


def _gather_kernel(cnt_ref, rows_ref, idx_ref, src, out_ref, idx_s, sem_i, sem_g):
    g = pl.program_id(0)
    n = cnt_ref[0]
    u = GATHER_UNROLL

    @pl.when(g < n)
    def _():
        rows = rows_ref[g]
        cp = pltpu.make_async_copy(idx_ref, idx_s, sem_i)
        cp.start()

        @pl.when(rows < out_ref.shape[0])
        def _():
            out_ref[...] = jnp.zeros(out_ref.shape, out_ref.dtype)

        cp.wait()

        def fetch(i, j):
            pltpu.make_async_copy(src.at[pl.ds(idx_s[0, i, j], 1)], out_ref.at[pl.ds(i * u + j, 1)], sem_g).start()

        nfull = rows // u

        def full(i, carry):
            for j in range(u):
                fetch(i, j)
            return carry

        lax.fori_loop(0, nfull, full, 0)

        def tail(j, carry):
            fetch(nfull, j)
            return carry

        lax.fori_loop(0, rows - nfull * u, tail, 0)

        @pl.when(rows > 0)
        def _():
            pltpu.make_async_copy(out_ref.at[pl.ds(0, rows)], out_ref.at[pl.ds(0, rows)], sem_g).wait()


def _gather_rows(src, idx, cnt, rows, ch):
    p = idx.shape[0]
    steps = p // ch
    u = GATHER_UNROLL
    nb = ch // u
    return pl.pallas_call(
        _gather_kernel,
        grid_spec=pltpu.PrefetchScalarGridSpec(
            num_scalar_prefetch=2,
            grid=(steps,),
            in_specs=[pl.BlockSpec((1, nb, u), lambda g, c, r: (g, 0, 0)), pl.BlockSpec(memory_space=pl.ANY)],
            out_specs=pl.BlockSpec((ch,) + src.shape[1:], lambda g, c, r: (jnp.minimum(g, c[0] - 1), 0, 0)),
            scratch_shapes=[pltpu.SMEM((1, nb, u), jnp.int32), pltpu.SemaphoreType.DMA, pltpu.SemaphoreType.DMA]),
        out_shape=jax.ShapeDtypeStruct((p,) + src.shape[1:], src.dtype),
        compiler_params=_cp(("arbitrary",)),
        name="row_gather",
    )(cnt, rows, idx.reshape(steps, nb, u), src)


def _combine_kernel(y0_ref, y1_ref, y2_ref, y3_ref, tg_ref, h_ref, g_ref, b_ref, o_ref, ffn):
    y_refs = (y0_ref, y1_ref, y2_ref, y3_ref)

    def rows8(i, carry):
        r0 = pl.multiple_of(i * SUBLANES, SUBLANES)
        tg = tg_ref[pl.ds(r0, SUBLANES), :]
        for t0 in range(0, ROW_TILES_Y, SUBLANES):
            acc = None
            for k, y_ref in enumerate(y_refs):
                pieces = _tiles_to_rows(y_ref[pl.ds(r0, SUBLANES), t0:t0 + SUBLANES, :])
                gk = tg[:, k:k + 1]
                acc = [gk * p for p in pieces] if acc is None else [a + gk * p for a, p in zip(acc, pieces)]
            for s in range(SUBLANES):
                ffn[pl.ds(r0, SUBLANES), (t0 + s) * LANES:(t0 + s + 1) * LANES] = acc[s]
        return carry

    lax.fori_loop(0, h_ref.shape[0] // SUBLANES, rows8, 0, unroll=2)
    o_ref[...] = _layer_norm(DEEPNORM_ALPHA * h_ref[...] + ffn[...], g_ref[...], b_ref[...])


def _combine(yg, tg, h, g, b, tm):
    t, d = h.shape
    ysp = lambda k: pl.BlockSpec((None, tm, ROW_TILES_Y, LANES), lambda i: (k, i, 0, 0))
    return pl.pallas_call(
        _combine_kernel,
        grid=(pl.cdiv(t, tm),),
        in_specs=[ysp(0), ysp(1), ysp(2), ysp(3),
                  pl.BlockSpec((tm, LANES), lambda i: (i, 0)),
                  pl.BlockSpec((tm, d), lambda i: (i, 0)),
                  pl.BlockSpec((1, d), lambda i: (0, 0)),
                  pl.BlockSpec((1, d), lambda i: (0, 0))],
        out_specs=pl.BlockSpec((tm, d), lambda i: (i, 0)),
        out_shape=jax.ShapeDtypeStruct((t, d), F32),
        scratch_shapes=[pltpu.VMEM((tm, d), F32)],
        compiler_params=_cp(("arbitrary",)),
        name="combine_ln2",
    )(yg, yg, yg, yg, tg, h, g, b)


def _lane_row(vals, offset, width=LANES):
    row = jnp.zeros((width,), F32)
    return lax.dynamic_update_slice(row, vals.astype(F32), (offset,)).reshape(1, width)


def _layer(x_all, P, past, nb, seq, nd, kv_prev):
    tp = nb * seq
    n_tok = tp + nd
    rb = tp // nd
    w_in = _prep_w_in(P['w_in'])
    hp_ = _matmul(x_all, w_in, 512, IN_TN)
    hs_ = hp_[tp:]

    hp = jnp.concatenate([_lane_row(P['ssd_dt_bias'], MISC_DT), _lane_row(P['ssd_a_log'], MISC_DT),
                          _lane_row(P['ssd_d'], MISC_DT), _lane_row(P['b_forget'], MISC_F),
                          jnp.zeros((4, LANES), F32)], axis=0)
    bf_row = hp[3:4]
    r1 = lambda a: a.reshape(1, -1).astype(F32)

    logf, c = _fox_prep(hp_, bf_row, nb, seq)
    a_out = _fox_prompt(hp_, c, nb, seq)
    b_out, ssm_p, sconv_p = _ssd_prompt(hp_, P['ssd_conv_w'], r1(P['ssd_conv_b']), hp, nb, seq)
    bst = jnp.pad(P['gmlp_bs'].T, ((0, 0), (0, LANES - GMLP_GROUPS)))
    c_out = _gmlp_prompt(hp_, r1(P['gmlp_ln_g']), r1(P['gmlp_ln_b']), P['gmlp_ws'], bst, tp)
    ccw = jnp.pad(P['conf_conv_w'], ((0, CONF_HALO - CONF_KERNEL), (0, 0)))
    d_out, conf_p = _conf_prompt(hp_, ccw, r1(P['conf_conv_b']), r1(P['conf_ln_g']), r1(P['conf_ln_b']), nb, seq)

    kv = _kv_state(hp_, tp, past['layer'], past['depth'], kv_prev)
    logf_p = logf[:, :FOX_HEADS].reshape(nb, seq, FOX_HEADS)
    state_p = (logf_p, ssm_p, sconv_p[:, SUBLANES - (SSD_CONV - 1):, :],
               conf_p[:, CONF_HALO - (CONF_KERNEL - 1):, :])

    w00 = jnp.repeat(P['gmlp_ws'][:, 0, 0], GROUP_W // GMLP_GROUPS).reshape(1, GROUP_W)
    b0 = jnp.repeat(P['gmlp_bs'][:, 0], GROUP_W // GMLP_GROUPS).reshape(1, GROUP_W)
    scw = jnp.pad(P['ssd_conv_w'], ((0, SUBLANES - SSD_CONV), (0, 0)))
    c_s, vrow_s, d_s, glu_s, xc_s, sm_s = _dec_rows(
        hp_, rb, nd, r1(P['gmlp_ln_g']), r1(P['gmlp_ln_b']), w00, b0,
        ccw, r1(P['conf_conv_b']), r1(P['conf_ln_g']), r1(P['conf_ln_b']), past['state_conv'],
        scw, r1(P['ssd_conv_b']), past['state_ssd_conv'], hp)
    yt, ssm_s = _dec_ssm(xc_s[:, :GROUP_W].T, hs_[:, COL_Z:COL_Z + GROUP_W].T, xc_s[:, GROUP_W:], sm_s, hp,
                         past['state_ssm'])
    b_s = yt.T
    a_s = _fox_decode(hp_, rb, sm_s, past['cache_k'], past['cache_v'], past['cache_lft'], past['page_table'],
                      past['layer'])
    k_s = hs_[:, COL_K:COL_K + GROUP_W].reshape(nd, 1, FOX_HEADS, FOX_HEAD_DIM)
    v_s = hs_[:, COL_V:COL_V + GROUP_W].reshape(nd, 1, FOX_HEADS, FOX_HEAD_DIM)
    logf_s = sm_s[:, :FOX_HEADS].reshape(nd, 1, FOX_HEADS)
    sconv_s = jnp.concatenate([past['state_ssd_conv'][:, 1:], hs_[:, None, COL_XBC:COL_XBC + SSD_CONV_DIM]], axis=1)
    conf_s = jnp.concatenate([past['state_conv'][:, 1:], glu_s[:, None, :]], axis=1)
    state_s = (k_s, v_s, logf_s, ssm_s, sconv_s, conf_s, vrow_s.reshape(nd, 1, GROUP_W))

    w_out = P['w_out'].astype(BF16)
    rw = jnp.pad(P['router_w'], ((0, 0), (0, LANES - N_EXPERTS)))
    rb = jnp.concatenate([P['router_b'].astype(F32), jnp.full((LANES - N_EXPERTS,), NEG, F32)]).reshape(1, LANES)
    op_args = (w_out, r1(P['mix_norm_g']), r1(P['ln1_g']), r1(P['ln1_b']), rw, rb)
    outs = _outproj(a_out, b_out, c_out, d_out, x_all, *op_args, tm=256, row0=0)
    h_all, hpk_all, ti_all, tg_all = _outproj(a_s, b_s, c_s, d_s, x_all, *op_args, tm=nd, row0=tp, prev=outs)

    row_tok, dest, ge, gb, gr, total = _moe_plan(ti_all[:, :TOP_K], n_tok)
    x_rows = _gather_rows(hpk_all, row_tok, total.reshape(1), gr, MOE_R)
    y_rows = _moe_ffn(x_rows, past['exp_w_gu'], past['exp_b_gu'], past['exp_w_dn'], past['exp_b_dn'], past['layer'],
                      ge, gb, gr)
    n_pairs = n_tok * TOP_K
    steps = max(s for s in range(1, 33) if (n_pairs // SUBLANES) % s == 0)
    yg = _gather_rows(y_rows, dest.T.reshape(-1), jnp.full((1,), steps, jnp.int32),
                      jnp.full((steps,), n_pairs // steps, jnp.int32), n_pairs // steps)
    o_all = _combine(yg.reshape(TOP_K, n_tok, ROW_TILES_Y, LANES), tg_all, h_all, r1(P['ln2_g']), r1(P['ln2_b']),
                     tm=256)
    return o_all, state_p, state_s, kv


PARAM_NAMES = ('w_in', 'b_forget', 'ssd_conv_w', 'ssd_conv_b', 'ssd_dt_bias', 'ssd_a_log', 'ssd_d', 'gmlp_ln_g',
               'gmlp_ln_b', 'gmlp_ws', 'gmlp_bs', 'conf_conv_w', 'conf_conv_b', 'conf_ln_g', 'conf_ln_b',
               'mix_norm_g', 'w_out', 'ln1_g', 'ln1_b', 'router_w', 'router_b', 'exp_w_gu', 'exp_b_gu',
               'exp_w_dn', 'exp_b_dn', 'ln2_g', 'ln2_b')


def kernel(x_prompt, x_sample, cache_k, cache_v, cache_logf, state_ssm, state_ssd_conv, state_conv, page_table, w_in, b_forget, ssd_conv_w, ssd_conv_b, ssd_dt_bias, ssd_a_log, ssd_d, gmlp_ln_g, gmlp_ln_b, gmlp_ws, gmlp_bs, conf_conv_w, conf_conv_b, conf_ln_g, conf_ln_b, mix_norm_g, w_out, ln1_g, ln1_b, router_w, router_b, exp_w_gu, exp_b_gu, exp_w_dn, exp_b_dn, ln2_g, ln2_b):
    weights = (w_in, b_forget, ssd_conv_w, ssd_conv_b, ssd_dt_bias, ssd_a_log, ssd_d, gmlp_ln_g, gmlp_ln_b, gmlp_ws,
               gmlp_bs, conf_conv_w, conf_conv_b, conf_ln_g, conf_ln_b, mix_norm_g, w_out, ln1_g, ln1_b, router_w,
               router_b, exp_w_gu, exp_b_gu, exp_w_dn, exp_b_dn, ln2_g, ln2_b)
    nb, seq, d = x_prompt.shape
    nd = x_sample.shape[0]
    depth = w_in.shape[0]
    n_pool = cache_k.shape[1]
    ck = cache_k.reshape(depth, n_pool, PAGE_SIZE, GROUP_W)
    cv = cache_v.reshape(depth, n_pool, PAGE_SIZE, GROUP_W)
    clft = jnp.swapaxes(cache_logf, 2, 3)
    tp = nb * seq
    x_all = jnp.concatenate([x_prompt.reshape(tp, d), x_sample.reshape(nd, d)], axis=0)
    st_p, st_s = [], []
    kv = None
    big = ('exp_w_gu', 'exp_b_gu', 'exp_w_dn', 'exp_b_dn')
    for l in range(depth):
        P = {n: w[l] for n, w in zip(PARAM_NAMES, weights) if n not in big}
        past = dict(cache_k=ck, cache_v=cv, cache_lft=clft, layer=l, depth=depth, state_ssm=state_ssm[l],
                    state_ssd_conv=state_ssd_conv[l], state_conv=state_conv[l], page_table=page_table,
                    exp_w_gu=exp_w_gu, exp_b_gu=exp_b_gu, exp_w_dn=exp_w_dn, exp_b_dn=exp_b_dn)
        x_all, sp, ss, kv = _layer(x_all, P, past, nb, seq, nd, kv)
        st_p.append(sp)
        st_s.append(ss)
    stack = lambda sts, i: jnp.stack([s[i] for s in sts], axis=0)
    kv_shape = (depth, nb, seq, FOX_HEADS, FOX_HEAD_DIM)
    return (x_all[:tp].reshape(nb, seq, d), x_all[tp:].reshape(nd, 1, d),
            kv[0].reshape(kv_shape), kv[1].reshape(kv_shape), stack(st_p, 0),
            stack(st_s, 0), stack(st_s, 1), stack(st_s, 2),
            stack(st_p, 1), stack(st_s, 3), stack(st_p, 2), stack(st_s, 4),
            stack(st_p, 3), stack(st_s, 5), stack(st_s, 6))
```

```python
import functools
import math

import jax
import jax.numpy as jnp
from jax import lax
from jax.experimental import pallas as pl
from jax.experimental.pallas import tpu as pltpu

F32 = jnp.float32
BF16 = jnp.bfloat16

D_MODEL = 2048
GROUP_W = 512
N_MIXERS = 4
FOX_HEADS = 8
FOX_HEAD_DIM = 64
SSD_HEADS = 8
SSD_HEAD_DIM = 64
SSD_STATE = 128
SSD_GROUPS = 2
SSD_CONV = 4
SSD_CHUNK = 128
SSD_CONV_DIM = GROUP_W + 2 * SSD_GROUPS * SSD_STATE
GMLP_CHUNK = 128
GMLP_GROUPS = 4
CONF_KERNEL = 31
IN_SIZES = (GROUP_W, GROUP_W, GROUP_W, FOX_HEADS, GROUP_W, SSD_CONV_DIM, SSD_HEADS,
            GROUP_W, GROUP_W, GROUP_W, GROUP_W)
IN_SPLITS = tuple(sum(IN_SIZES[:i + 1]) for i in range(len(IN_SIZES) - 1))
N_EXPERTS = 32
TOP_K = 4
D_FF = D_MODEL
SWIGLU_LIMIT = 7.0
SWIGLU_ALPHA = 1.702
DEPTH = 2
DEEPNORM_ALPHA = (2 * DEPTH) ** 0.25
EPS = 1e-5
PAGE_SIZE = 128

LANES = 128
SUBLANES = 8
VMEM_LIMIT = 56 * 1024 * 1024

COL_Q, COL_K, COL_V, COL_Z, COL_XBC, COL_GU, COL_GV, COL_GA, COL_GB, COL_MISC = (
    0, 512, 1024, 1536, 2048, 3072, 3584, 4096, 4608, 5120)
MISC_F = 0
MISC_DT = 8
IN_COLS_PAD = 5376
IN_TN = 1792
NEG = -1e30

MOE_R = 1152
MOE_SMALL = 384
MOE_TF = 256
ROW_TILES_X = D_MODEL // 2 // LANES
ROW_TILES_Y = D_MODEL // LANES


def _cp(sem, vmem=VMEM_LIMIT):
    return pltpu.CompilerParams(dimension_semantics=sem, vmem_limit_bytes=vmem)


def _split3(x):
    hi = x.astype(BF16)
    r1 = x - hi.astype(F32)
    mid = r1.astype(BF16)
    lo = (r1 - mid.astype(F32)).astype(BF16)
    return hi, mid, lo


def _dot01_left(t01, x):
    hi, mid, lo = _split3(x)
    d = lambda b: jnp.dot(t01, b, preferred_element_type=F32)
    return d(hi) + d(mid) + d(lo)


def _dot01_right(x, t01):
    hi, mid, lo = _split3(x)
    d = lambda a: jnp.dot(a, t01, preferred_element_type=F32)
    return d(hi) + d(mid) + d(lo)


def _log_sigmoid(x):
    return jnp.minimum(x, 0.0) - jnp.log(1.0 + jnp.exp(-jnp.abs(x)))


def _softplus(x):
    return jnp.maximum(x, 0.0) + jnp.log(1.0 + jnp.exp(-jnp.abs(x)))


def _sigmoid(x):
    return 1.0 / (1.0 + jnp.exp(-x))


def _silu(x):
    return x * _sigmoid(x)


def _gelu(x):
    return 0.5 * x * (1.0 + jnp.tanh(math.sqrt(2.0 / math.pi) * (x + 0.044715 * (x * x * x))))


def _layer_norm(x, g, b):
    mu = jnp.mean(x, axis=-1, keepdims=True)
    xc = x - mu
    var = jnp.mean(xc * xc, axis=-1, keepdims=True)
    return xc * lax.rsqrt(var + EPS) * g + b


def _mm_kernel(x_ref, w_ref, o_ref):
    o_ref[...] = jnp.dot(x_ref[...].astype(BF16), w_ref[...], preferred_element_type=F32)


def _matmul(x, w, tm, tn):
    m, k = x.shape
    n = w.shape[1]
    return pl.pallas_call(
        _mm_kernel,
        grid=(n // tn, pl.cdiv(m, tm)),
        in_specs=[pl.BlockSpec((tm, k), lambda j, i: (i, 0)),
                  pl.BlockSpec((k, tn), lambda j, i: (0, j))],
        out_specs=pl.BlockSpec((tm, tn), lambda j, i: (i, j)),
        out_shape=jax.ShapeDtypeStruct((m, n), F32),
        compiler_params=_cp(("arbitrary", "arbitrary")),
        name="in_proj",
    )(x, w)


def _prep_w_in(w):
    q, k, v, f, z, xbc, dt, gu, gv, ga, gb = jnp.split(w, IN_SPLITS, axis=-1)
    d = w.shape[0]
    misc = jnp.concatenate([f, dt, jnp.zeros((d, LANES - 16), w.dtype)], axis=-1)
    pad = jnp.zeros((d, IN_COLS_PAD - (COL_MISC + LANES)), w.dtype)
    return jnp.concatenate([q, k, v, z, xbc, gu, gv, ga, gb, misc, pad], axis=-1).astype(BF16)


FOX_CH = 256


def _fox_prep_kernel(misc_ref, bf_ref, lf_ref, c_ref, carry):
    ch = misc_ref.shape[0]

    @pl.when(pl.program_id(1) == 0)
    def _():
        carry[...] = jnp.zeros_like(carry)

    lf = _log_sigmoid(misc_ref[...] + bf_ref[...])
    row = lax.broadcasted_iota(jnp.int32, (ch, ch), 0)
    col = lax.broadcasted_iota(jnp.int32, (ch, ch), 1)
    tri = jnp.where(row >= col, 1.0, 0.0).astype(BF16)
    cs = _dot01_left(tri, lf) + carry[...]
    carry[...] = cs[ch - 1:ch, :]
    lf_ref[...] = lf
    c_ref[...] = cs


def _fox_prep(h, bf_row, nb, seq):
    ch = FOX_CH
    nch = seq // ch
    t = nb * seq
    return pl.pallas_call(
        _fox_prep_kernel,
        grid=(nb, nch),
        in_specs=[pl.BlockSpec((ch, LANES), lambda b, c: (b * nch + c, COL_MISC // LANES)),
                  pl.BlockSpec((1, LANES), lambda b, c: (0, 0))],
        out_specs=[pl.BlockSpec((ch, LANES), lambda b, c: (b * nch + c, 0)),
                   pl.BlockSpec((ch, LANES), lambda b, c: (b * nch + c, 0))],
        out_shape=[jax.ShapeDtypeStruct((t, LANES), F32),
                   jax.ShapeDtypeStruct((t, LANES), F32)],
        scratch_shapes=[pltpu.VMEM((1, LANES), F32)],
        compiler_params=_cp(("arbitrary", "arbitrary")),
        name="fox_prep",
    )(h, bf_row)


def _fox_bias_lanes(c_col, q_side):
    hi, mid, lo = (t.astype(F32) for t in _split3(c_col))
    terms = (hi, mid, lo, 1.0, 1.0, 1.0) if q_side else (1.0, 1.0, 1.0, -hi, -mid, -lo)
    lane = lax.broadcasted_iota(jnp.int32, (1, FOX_HEAD_DIM), 1)
    ext = jnp.zeros((c_col.shape[0], FOX_HEAD_DIM), F32)
    for i, t in enumerate(terms):
        ext = jnp.where(lane == i, t, ext)
    return ext


def _fox_kernel(q_ref, k_ref, v_ref, c_ref, o_ref, kaug, vt, sc):
    tq = q_ref.shape[0]
    nblk = k_ref.shape[0] // tq
    qi = pl.program_id(1)
    hd = FOX_HEAD_DIM

    @pl.when(qi == 0)
    def _():
        for j in range(nblk):
            rows = slice(j * tq, (j + 1) * tq)
            kblk = k_ref[rows, :]
            cblk = c_ref[rows, :]
            for h in range(FOX_HEADS):
                kaug[h, rows, :] = jnp.concatenate(
                    [kblk[:, h * hd:(h + 1) * hd], _fox_bias_lanes(cblk[:, h:h + 1], False)], axis=1).astype(BF16)
            vt[j] = v_ref[rows, :].T.astype(BF16)

    qblk = q_ref[...] * (hd ** -0.5)
    cq = c_ref[pl.ds(pl.multiple_of(qi * tq, tq), tq), :]
    qaug = [jnp.concatenate([qblk[:, h * hd:(h + 1) * hd], _fox_bias_lanes(cq[:, h:h + 1], True)], axis=1).astype(BF16)
            for h in range(FOX_HEADS)]
    key = lax.broadcasted_iota(jnp.int32, (tq, tq), 0)
    qry = lax.broadcasted_iota(jnp.int32, (tq, tq), 1)
    causal = key <= qry
    nt = (((1,), (1,)), ((), ()))

    def score(j, stats, masked):
        start = pl.multiple_of(j * tq, tq)
        new = []
        for h in range(FOX_HEADS):
            m, l = stats[h]
            st = lax.dot_general(kaug[h, pl.ds(start, tq), :], qaug[h], nt, preferred_element_type=F32)
            if masked:
                st = jnp.where(causal, st, NEG)
            sc[h, j] = st
            m_new = jnp.maximum(m, jnp.max(st, axis=0, keepdims=True))
            l = jnp.exp(m - m_new) * l + jnp.sum(jnp.exp(st - m_new), axis=0, keepdims=True)
            new.append((m_new, l))
        return tuple(new)

    stats = tuple((jnp.full((1, tq), NEG, F32), jnp.zeros((1, tq), F32)) for _ in range(FOX_HEADS))
    stats = lax.fori_loop(0, qi, lambda j, c: score(j, c, False), stats)
    stats = score(qi, stats, True)

    def apply(j, accs):
        new = []
        for h in range(FOX_HEADS):
            m, l = stats[h]
            p = jnp.exp(sc[h, j] - m) / l
            new.append(accs[h] + jnp.dot(vt[j, h * hd:(h + 1) * hd, :], p.astype(BF16), preferred_element_type=F32))
        return tuple(new)

    accs = lax.fori_loop(0, qi + 1, apply, tuple(jnp.zeros((hd, tq), F32) for _ in range(FOX_HEADS)))
    o_ref[...] = jnp.concatenate(accs, axis=0).T


def _fox_prompt(h, c, nb, seq):
    tq = FOX_CH
    nq = seq // tq
    t = nb * seq
    return pl.pallas_call(
        _fox_kernel,
        grid=(nb, nq),
        in_specs=[pl.BlockSpec((tq, GROUP_W), lambda b, i: (b * nq + i, COL_Q // GROUP_W)),
                  pl.BlockSpec((seq, GROUP_W), lambda b, i: (b, COL_K // GROUP_W)),
                  pl.BlockSpec((seq, GROUP_W), lambda b, i: (b, COL_V // GROUP_W)),
                  pl.BlockSpec((seq, LANES), lambda b, i: (b, 0))],
        out_specs=pl.BlockSpec((tq, GROUP_W), lambda b, i: (b * nq + i, 0)),
        out_shape=jax.ShapeDtypeStruct((t, GROUP_W), F32),
        scratch_shapes=[pltpu.VMEM((FOX_HEADS, seq, 2 * FOX_HEAD_DIM), BF16),
                        pltpu.VMEM((nq, GROUP_W, tq), BF16),
                        pltpu.VMEM((FOX_HEADS, nq, tq, tq), F32)],
        compiler_params=_cp(("arbitrary", "arbitrary")),
        name="fox_prompt",
    )(h, h, h, c)


def _kv_state_kernel(k_ref, v_ref, *rest):
    ko_ref, vo_ref = rest[-2:]
    k = k_ref[...]
    v = v_ref[...]
    for h in range(FOX_HEADS):
        ko_ref[:, h, :] = k[:, h * FOX_HEAD_DIM:(h + 1) * FOX_HEAD_DIM]
        vo_ref[:, h, :] = v[:, h * FOX_HEAD_DIM:(h + 1) * FOX_HEAD_DIM]


def _kv_state(h, t, layer, depth, prev=None):
    tm = 512
    prev = () if prev is None else tuple(prev)
    osp = pl.BlockSpec((None, tm, FOX_HEADS, FOX_HEAD_DIM), lambda i: (layer, i, 0, 0))
    osd = jax.ShapeDtypeStruct((depth, t, FOX_HEADS, FOX_HEAD_DIM), F32)
    return pl.pallas_call(
        _kv_state_kernel,
        grid=(t // tm,),
        in_specs=[pl.BlockSpec((tm, GROUP_W), lambda i: (i, COL_K // GROUP_W)),
                  pl.BlockSpec((tm, GROUP_W), lambda i: (i, COL_V // GROUP_W))]
                 + [pl.BlockSpec(memory_space=pl.ANY)] * len(prev),
        out_specs=[osp, osp],
        out_shape=[osd, osd],
        input_output_aliases={2 + i: i for i in range(len(prev))},
        compiler_params=_cp(("arbitrary",)),
        name="kv_state",
    )(h, h, *prev)


DEC_PPS = 16


def _fox_dec_kernel(pt_ref, q_ref, kn_ref, vn_ref, lfn_ref, *rest):
    del pt_ref
    pps = DEC_PPS
    k_refs = rest[:pps]
    v_refs = rest[pps:2 * pps]
    lf_refs = rest[2 * pps:3 * pps]
    o_ref = rest[3 * pps]
    m_s, l_s, acc_s, carry_s, qt_s, snew_s, lg_s, lfpad = rest[3 * pps + 1:]
    b = pl.program_id(0)
    s = pl.program_id(1)
    ns = pl.num_programs(1) // 2
    head_lane = lax.broadcasted_iota(jnp.int32, (1, LANES), 1) < FOX_HEADS
    erow = lax.broadcasted_iota(jnp.int32, (LANES, GROUP_W), 0)
    ecol = lax.broadcasted_iota(jnp.int32, (LANES, GROUP_W), 1) // FOX_HEAD_DIM
    expand = jnp.where(erow == ecol, 1.0, 0.0).astype(BF16)
    bcast8 = lambda row: jnp.broadcast_to(row, (SUBLANES, row.shape[1]))

    @pl.when(s == 0)
    def _():
        hrow = lax.broadcasted_iota(jnp.int32, (LANES, GROUP_W), 0)
        hcol = lax.broadcasted_iota(jnp.int32, (LANES, GROUP_W), 1) // FOX_HEAD_DIM
        qrow = q_ref[pl.ds(b, 1), :] * (FOX_HEAD_DIM ** -0.5)
        qm = jnp.where(hrow == hcol, jnp.broadcast_to(qrow, (LANES, GROUP_W)), 0.0)
        qt_s[...] = qm.T.astype(BF16)
        kn = bcast8(kn_ref[pl.ds(b, 1), :]).astype(BF16)
        snew = jnp.dot(kn, qt_s[...], preferred_element_type=F32)[0:1, :]
        snew_s[...] = snew
        m_s[...] = jnp.where(head_lane, snew, NEG)
        carry_s[...] = jnp.where(head_lane, lfn_ref[pl.ds(b, 1), :], 0.0)
        lfpad[...] = jnp.zeros_like(lfpad)

    @pl.when(s < ns)
    def _():
        r = lax.broadcasted_iota(jnp.int32, (PAGE_SIZE, PAGE_SIZE), 0)
        c = lax.broadcasted_iota(jnp.int32, (PAGE_SIZE, PAGE_SIZE), 1)
        later = jnp.where(c > r, 1.0, 0.0).astype(BF16)
        qt = qt_s[...]
        m, carry = m_s[...], carry_s[...]
        for i in range(pps):
            kp = k_refs[i][...].astype(BF16)
            lfpad[:, 0:FOX_HEADS] = lf_refs[i][...]
            lf = lfpad[...]
            st = jnp.dot(kp, qt, preferred_element_type=F32)
            logits = jnp.where(head_lane, st + (_dot01_left(later, lf) + carry), NEG)
            carry = carry + jnp.sum(lf, axis=0, keepdims=True)
            m = jnp.maximum(m, jnp.max(logits, axis=0, keepdims=True))
            lg_s[s * pps + i] = logits
        m_s[...], carry_s[...] = m, carry

    @pl.when(s == ns - 1)
    def _():
        m = m_s[...]

        def add(j, l):
            return l + jnp.sum(jnp.exp(lg_s[j] - m), axis=0, keepdims=True)

        l_s[...] = lax.fori_loop(0, ns * pps, add, jnp.where(head_lane, jnp.exp(snew_s[...] - m), 0.0))
        acc_s[...] = jnp.zeros_like(acc_s)

    @pl.when(s >= ns)
    def _():
        m, l = m_s[...], jnp.where(head_lane, l_s[...], 1.0)
        acc = acc_s[...]
        for i in range(pps):
            vp = v_refs[i][...].astype(BF16).astype(F32)
            p = (jnp.exp(lg_s[(s - ns) * pps + i] - m) / l).astype(BF16)
            pe = jnp.dot(p, expand, preferred_element_type=F32)
            acc = acc + jnp.sum((pe * vp).reshape(PAGE_SIZE // SUBLANES, SUBLANES, GROUP_W), axis=0)
        acc_s[...] = acc

    @pl.when(s == 2 * ns - 1)
    def _():
        m, l = m_s[...], jnp.where(head_lane, l_s[...], 1.0)
        pn = jnp.where(head_lane, jnp.exp(snew_s[...] - m) / l, 0.0).astype(BF16)
        pne = jnp.dot(bcast8(pn), expand, preferred_element_type=F32)[0:1, :]
        vn = vn_ref[pl.ds(b, 1), :].astype(BF16).astype(F32)
        o_ref[...] = jnp.sum(acc_s[...], axis=0, keepdims=True) + pne * vn


def _fox_decode(hs, rb, lfn, ck, cv, clft, page_table, layer):
    nb, npg = page_table.shape
    pps = DEC_PPS
    ns = npg // pps
    pt = page_table.reshape(-1)

    def page_map(i, phase):
        def index(b, s, pt_ref):
            sp = jnp.minimum(s, ns - 1) if phase == 0 else jnp.maximum(s - ns, 0)
            return (layer, pt_ref[b * npg + (npg - 1 - (sp * pps + i))], 0, 0)
        return index

    full = lambda cb: pl.BlockSpec((nb, GROUP_W), lambda b, s, pt_ref: (rb, cb))
    in_specs = [full(COL_Q // GROUP_W), full(COL_K // GROUP_W), full(COL_V // GROUP_W),
                pl.BlockSpec((nb, LANES), lambda b, s, pt_ref: (0, 0))]
    in_specs += [pl.BlockSpec((None, None, PAGE_SIZE, GROUP_W), page_map(i, 0)) for i in range(pps)]
    in_specs += [pl.BlockSpec((None, None, PAGE_SIZE, GROUP_W), page_map(i, 1)) for i in range(pps)]
    in_specs += [pl.BlockSpec((None, None, PAGE_SIZE, FOX_HEADS), page_map(i, 0)) for i in range(pps)]
    out = pl.pallas_call(
        _fox_dec_kernel,
        grid_spec=pltpu.PrefetchScalarGridSpec(
            num_scalar_prefetch=1,
            grid=(nb, 2 * ns),
            in_specs=in_specs,
            out_specs=pl.BlockSpec((None, 1, GROUP_W), lambda b, s, pt_ref: (b, 0, 0)),
            scratch_shapes=[pltpu.VMEM((1, LANES), F32), pltpu.VMEM((1, LANES), F32),
                            pltpu.VMEM((SUBLANES, GROUP_W), F32), pltpu.VMEM((1, LANES), F32),
                            pltpu.VMEM((GROUP_W, LANES), BF16), pltpu.VMEM((1, LANES), F32),
                            pltpu.VMEM((npg, PAGE_SIZE, LANES), F32), pltpu.VMEM((PAGE_SIZE, LANES), F32)]),
        out_shape=jax.ShapeDtypeStruct((nb, 1, GROUP_W), F32),
        compiler_params=_cp(("arbitrary", "arbitrary")),
        name="fox_decode",
    )(pt, hs, hs, hs, lfn, *([ck] * pps), *([cv] * pps), *([clft] * pps))
    return out.reshape(nb, GROUP_W)


def _ssd_kernel(xbc_ref, z_ref, misc_ref, cw_ref, cb_ref, hp_ref, o_ref, st_ref, cs_ref, carry, hstate):
    q = SSD_CHUNK
    c = pl.program_id(1)
    nc = pl.num_programs(1)

    @pl.when(c == 0)
    def _():
        carry[...] = jnp.zeros_like(carry)
        hstate[...] = jnp.zeros_like(hstate)

    xin = xbc_ref[...]
    full = jnp.concatenate([carry[...], xin], axis=0).astype(BF16).astype(F32)
    cw = cw_ref[...].astype(BF16).astype(F32)
    conv = cb_ref[...] + sum(full[SUBLANES - (SSD_CONV - 1) + w:SUBLANES - (SSD_CONV - 1) + w + q, :] * cw[w:w + 1, :]
                             for w in range(SSD_CONV))
    carry[...] = xin[q - SUBLANES:q, :]
    cs_ref[...] = xin[q - SUBLANES:q, :]
    xc = _silu(conv)
    xs = xc[:, :GROUP_W]

    dt = _softplus(misc_ref[...] + hp_ref[0:1, :])
    a_row = -jnp.exp(hp_ref[1:2, :])
    d_row = hp_ref[2:3, :]
    da = dt * a_row
    row = lax.broadcasted_iota(jnp.int32, (q, q), 0)
    col = lax.broadcasted_iota(jnp.int32, (q, q), 1)
    tril = row >= col
    tril01 = jnp.where(tril, 1.0, 0.0).astype(BF16)
    triu01 = jnp.where(row <= col, 1.0, 0.0).astype(BF16)
    acs = _dot01_left(tril01, da)
    dt_t = dt.T
    acs_t = _dot01_right(da.T, triu01)
    nt = (((1,), (1,)), ((), ()))
    tn = (((0,), (0,)), ((), ()))
    rep = SSD_HEADS // SSD_GROUPS
    for g in range(SSD_GROUPS):
        bm = xc[:, GROUP_W + g * SSD_STATE:GROUP_W + (g + 1) * SSD_STATE].astype(BF16)
        cm = xc[:, GROUP_W + (SSD_GROUPS + g) * SSD_STATE:GROUP_W + (SSD_GROUPS + g + 1) * SSD_STATE].astype(BF16)
        cbm = lax.dot_general(cm, bm, nt, preferred_element_type=F32)
        for r in range(rep):
            hd = g * rep + r
            ln = MISC_DT + hd
            lo, hi = hd * SSD_HEAD_DIM, (hd + 1) * SSD_HEAD_DIM
            a_col = acs[:, ln:ln + 1]
            a_rw = acs_t[ln:ln + 1, :]
            dt_rw = dt_t[ln:ln + 1, :]
            dt_col = dt[:, ln:ln + 1]
            a_last = acs[q - 1:q, ln:ln + 1]
            decay = jnp.where(tril, jnp.exp(jnp.minimum(a_col - a_rw, 0.0)), 0.0)
            mm = (cbm * decay * dt_rw).astype(BF16)
            xh = xs[:, lo:hi]
            y = jnp.dot(mm, xh.astype(BF16), preferred_element_type=F32)
            xw = (xh * (jnp.exp(a_last - a_col) * dt_col)).astype(BF16)
            states = lax.dot_general(xw, bm, tn, preferred_element_type=F32)
            hprev = hstate[hd]
            y_off = lax.dot_general(cm, hprev.astype(BF16), nt, preferred_element_type=F32)
            y = y + y_off * jnp.exp(a_col) + d_row[:, ln:ln + 1] * xh
            hstate[hd] = hprev * jnp.exp(a_last) + states
            o_ref[:, lo:hi] = y
    o_ref[...] = o_ref[...] * _silu(z_ref[...])

    @pl.when(c == nc - 1)
    def _():
        st_ref[...] = hstate[...]


def _ssd_prompt(h, cw, cb, hp, nb, seq):
    q = SSD_CHUNK
    nc = seq // q
    t = nb * seq
    return pl.pallas_call(
        _ssd_kernel,
        grid=(nb, nc),
        in_specs=[pl.BlockSpec((q, SSD_CONV_DIM), lambda b, c: (b * nc + c, COL_XBC // SSD_CONV_DIM)),
                  pl.BlockSpec((q, GROUP_W), lambda b, c: (b * nc + c, COL_Z // GROUP_W)),
                  pl.BlockSpec((q, LANES), lambda b, c: (b * nc + c, COL_MISC // LANES)),
                  pl.BlockSpec((SSD_CONV, SSD_CONV_DIM), lambda b, c: (0, 0)),
                  pl.BlockSpec((1, SSD_CONV_DIM), lambda b, c: (0, 0)),
                  pl.BlockSpec((SUBLANES, LANES), lambda b, c: (0, 0))],
        out_specs=[pl.BlockSpec((q, GROUP_W), lambda b, c: (b * nc + c, 0)),
                   pl.BlockSpec((None, SSD_HEADS, SSD_HEAD_DIM, SSD_STATE), lambda b, c: (b, 0, 0, 0)),
                   pl.BlockSpec((None, SUBLANES, SSD_CONV_DIM), lambda b, c: (b, 0, 0))],
        out_shape=[jax.ShapeDtypeStruct((t, GROUP_W), F32),
                   jax.ShapeDtypeStruct((nb, SSD_HEADS, SSD_HEAD_DIM, SSD_STATE), F32),
                   jax.ShapeDtypeStruct((nb, SUBLANES, SSD_CONV_DIM), F32)],
        scratch_shapes=[pltpu.VMEM((SUBLANES, SSD_CONV_DIM), F32),
                        pltpu.VMEM((SSD_HEADS, SSD_HEAD_DIM, SSD_STATE), F32)],
        compiler_params=_cp(("arbitrary", "arbitrary")),
        name="ssd_prompt",
    )(h, h, h, cw, cb, hp)


def _gmlp_kernel(u_ref, v_ref, lng_ref, lnb_ref, ws_ref, bst_ref, o_ref):
    n = GMLP_CHUNK
    v = _layer_norm(_gelu(v_ref[...]), lng_ref[...], lnb_ref[...])
    row = lax.broadcasted_iota(jnp.int32, (n, n), 0)
    col = lax.broadcasted_iota(jnp.int32, (n, n), 1)
    tril = row >= col
    gc = GROUP_W // GMLP_GROUPS
    parts = []
    for g in range(GMLP_GROUPS):
        wm = jnp.where(tril, ws_ref[g], 0.0).astype(BF16)
        sg = jnp.dot(wm, v[:, g * gc:(g + 1) * gc].astype(BF16), preferred_element_type=F32)
        parts.append(sg + bst_ref[:, g:g + 1])
    s = jnp.concatenate(parts, axis=-1)
    o_ref[...] = _gelu(u_ref[...]) * s


def _gmlp_prompt(h, lng, lnb, ws, bst, t):
    n = GMLP_CHUNK
    return pl.pallas_call(
        _gmlp_kernel,
        grid=(t // n,),
        in_specs=[pl.BlockSpec((n, GROUP_W), lambda i: (i, COL_GU // GROUP_W)),
                  pl.BlockSpec((n, GROUP_W), lambda i: (i, COL_GV // GROUP_W)),
                  pl.BlockSpec((1, GROUP_W), lambda i: (0, 0)),
                  pl.BlockSpec((1, GROUP_W), lambda i: (0, 0)),
                  pl.BlockSpec((GMLP_GROUPS, n, n), lambda i: (0, 0, 0)),
                  pl.BlockSpec((n, LANES), lambda i: (0, 0))],
        out_specs=pl.BlockSpec((n, GROUP_W), lambda i: (i, 0)),
        out_shape=jax.ShapeDtypeStruct((t, GROUP_W), F32),
        compiler_params=_cp(("arbitrary",)),
        name="gmlp_prompt",
    )(h, h, lng, lnb, ws, bst)


CONF_HALO = 32
CONF_CHUNK = 128


def _conf_kernel(a_ref, g_ref, cw_ref, cb_ref, lng_ref, lnb_ref, o_ref, st_ref, carry):
    n = CONF_CHUNK

    @pl.when(pl.program_id(1) == 0)
    def _():
        carry[...] = jnp.zeros_like(carry)

    glu = a_ref[...] * _sigmoid(g_ref[...])
    full = jnp.concatenate([carry[...], glu], axis=0).astype(BF16).astype(F32)
    cw = cw_ref[...].astype(BF16).astype(F32)
    off = CONF_HALO - (CONF_KERNEL - 1)
    acc = cb_ref[...] + full[off:off + n, :] * cw[0:1, :]
    for w in range(1, CONF_KERNEL):
        acc = acc + full[off + w:off + w + n, :] * cw[w:w + 1, :]
    carry[...] = glu[n - CONF_HALO:n, :]
    st_ref[...] = glu[n - CONF_HALO:n, :]
    o_ref[...] = _silu(_layer_norm(acc, lng_ref[...], lnb_ref[...]))


def _conf_prompt(h, cw, cb, lng, lnb, nb, seq):
    n = CONF_CHUNK
    nc = seq // n
    t = nb * seq
    return pl.pallas_call(
        _conf_kernel,
        grid=(nb, nc),
        in_specs=[pl.BlockSpec((n, GROUP_W), lambda b, c: (b * nc + c, COL_GA // GROUP_W)),
                  pl.BlockSpec((n, GROUP_W), lambda b, c: (b * nc + c, COL_GB // GROUP_W)),
                  pl.BlockSpec((CONF_HALO, GROUP_W), lambda b, c: (0, 0)),
                  pl.BlockSpec((1, GROUP_W), lambda b, c: (0, 0)),
                  pl.BlockSpec((1, GROUP_W), lambda b, c: (0, 0)),
                  pl.BlockSpec((1, GROUP_W), lambda b, c: (0, 0))],
        out_specs=[pl.BlockSpec((n, GROUP_W), lambda b, c: (b * nc + c, 0)),
                   pl.BlockSpec((None, CONF_HALO, GROUP_W), lambda b, c: (b, 0, 0))],
        out_shape=[jax.ShapeDtypeStruct((t, GROUP_W), F32),
                   jax.ShapeDtypeStruct((nb, CONF_HALO, GROUP_W), F32)],
        scratch_shapes=[pltpu.VMEM((CONF_HALO, GROUP_W), F32)],
        compiler_params=_cp(("arbitrary", "arbitrary")),
        name="conf_prompt",
    )(h, h, cw, cb, lng, lnb)


def _dec_rows_kernel(gu_ref, gv_ref, ga_ref, gb_ref, xbc_ref, misc_ref,
                     lng_ref, lnb_ref, w00_ref, b0_ref,
                     ccw_ref, ccb_ref, clg_ref, clb_ref, cprev_ref,
                     scw_ref, scb_ref, sprev_ref, hp_ref,
                     c_ref, vrow_ref, d_ref, glu_ref, xc_ref, sm_ref):
    nb = gu_ref.shape[0]
    v = _layer_norm(_gelu(gv_ref[...]), lng_ref[...], lnb_ref[...])
    vrow_ref[...] = v
    c_ref[...] = _gelu(gu_ref[...]) * (w00_ref[...] * v + b0_ref[...])
    glu = ga_ref[...] * _sigmoid(gb_ref[...])
    glu_ref[...] = glu
    kw = CONF_KERNEL - 1
    rows = []
    for b in range(nb):
        rows.append(jnp.sum(cprev_ref[b] * ccw_ref[0:kw, :], axis=0, keepdims=True))
    conv = jnp.concatenate(rows, axis=0) + glu * ccw_ref[kw:kw + 1, :] + ccb_ref[...]
    d_ref[...] = _silu(_layer_norm(conv, clg_ref[...], clb_ref[...]))
    sk = SSD_CONV - 1
    rows = []
    for b in range(nb):
        rows.append(jnp.sum(sprev_ref[b] * scw_ref[0:sk, :], axis=0, keepdims=True))
    sconv = jnp.concatenate(rows, axis=0) + xbc_ref[...] * scw_ref[sk:sk + 1, :] + scb_ref[...]
    xc_ref[...] = _silu(sconv)
    misc = misc_ref[...]
    lane = lax.broadcasted_iota(jnp.int32, misc.shape, 1)
    lf = _log_sigmoid(misc + hp_ref[3:4, :])
    dt = _softplus(misc + hp_ref[0:1, :])
    sm_ref[...] = jnp.where(lane < MISC_DT, lf, dt)


def _dec_rows(hs, rb, nb, lng, lnb, w00, b0, ccw, ccb, clg, clb, cprev, scw, scb, sprev, hp):
    col = lambda c0, w: pl.BlockSpec((nb, w), lambda i: (rb, c0 // w))
    whole = lambda a: pl.BlockSpec(a.shape, lambda i: (0,) * a.ndim)
    sd = lambda w: jax.ShapeDtypeStruct((nb, w), F32)
    return pl.pallas_call(
        _dec_rows_kernel,
        grid=(1,),
        in_specs=[col(COL_GU, GROUP_W), col(COL_GV, GROUP_W), col(COL_GA, GROUP_W), col(COL_GB, GROUP_W),
                  col(COL_XBC, SSD_CONV_DIM), col(COL_MISC, LANES)]
                 + [whole(a) for a in (lng, lnb, w00, b0, ccw, ccb, clg, clb, cprev, scw, scb, sprev, hp)],
        out_specs=[pl.BlockSpec((nb, GROUP_W), lambda i: (0, 0))] * 4
                  + [pl.BlockSpec((nb, SSD_CONV_DIM), lambda i: (0, 0)), pl.BlockSpec((nb, LANES), lambda i: (0, 0))],
        out_shape=[sd(GROUP_W)] * 4 + [sd(SSD_CONV_DIM), sd(LANES)],
        compiler_params=_cp(("arbitrary",)),
        name="dec_rows",
    )(hs, hs, hs, hs, hs, hs, lng, lnb, w00, b0, ccw, ccb, clg, clb, cprev, scw, scb, sprev, hp)


def _dec_ssm_kernel(xt_ref, zt_ref, bc_ref, sm_ref, hp_ref, h_ref, yt_ref, hn_ref):
    nb = bc_ref.shape[0]
    rep = SSD_HEADS // SSD_GROUPS
    a_row = -jnp.exp(hp_ref[1:2, :])
    d_row = hp_ref[2:3, :]
    sm = sm_ref[...]
    for b in range(nb):
        cols = []
        for hd in range(SSD_HEADS):
            g = hd // rep
            ln = MISC_DT + hd
            lo, hi = hd * SSD_HEAD_DIM, (hd + 1) * SSD_HEAD_DIM
            dt = sm[b:b + 1, ln:ln + 1]
            x = xt_ref[lo:hi, b:b + 1]
            brow = bc_ref[b:b + 1, g * SSD_STATE:(g + 1) * SSD_STATE]
            crow = bc_ref[b:b + 1, (SSD_GROUPS + g) * SSD_STATE:(SSD_GROUPS + g + 1) * SSD_STATE]
            hnew = h_ref[b, hd] * jnp.exp(dt * a_row[:, ln:ln + 1]) + (dt * x) * brow
            hn_ref[b, hd] = hnew
            y = jnp.sum(hnew.astype(BF16).astype(F32) * crow.astype(BF16).astype(F32), axis=-1, keepdims=True)
            cols.append(y + d_row[:, ln:ln + 1] * x)
        ycol = jnp.concatenate(cols, axis=0)
        yt_ref[:, b:b + 1] = ycol * _silu(zt_ref[:, b:b + 1])


def _dec_ssm(xt, zt, bc, sm, hp, hstate):
    nb = bc.shape[0]
    whole = lambda a: pl.BlockSpec(a.shape, lambda i: (0,) * a.ndim)
    return pl.pallas_call(
        _dec_ssm_kernel,
        grid=(1,),
        in_specs=[whole(a) for a in (xt, zt, bc, sm, hp, hstate)],
        out_specs=[pl.BlockSpec((GROUP_W, nb), lambda i: (0, 0)),
                   pl.BlockSpec(hstate.shape, lambda i: (0, 0, 0, 0))],
        out_shape=[jax.ShapeDtypeStruct((GROUP_W, nb), F32), jax.ShapeDtypeStruct(hstate.shape, F32)],
        compiler_params=_cp(("arbitrary",)),
        name="dec_ssm",
    )(xt, zt, bc, sm, hp, hstate)


def _rows_to_tiles(pieces):
    return jnp.swapaxes(jnp.stack(pieces, axis=0), 0, 1)


def _tiles_to_rows(block):
    t = jnp.swapaxes(block, 0, 1)
    return [t[s] for s in range(SUBLANES)]


def _pack_bf16_pair(lo, hi):
    lo_bits = lax.bitcast_convert_type(lo.astype(BF16).astype(F32), jnp.uint32)
    hi_bits = lax.bitcast_convert_type(hi.astype(BF16).astype(F32), jnp.uint32)
    return (lo_bits >> 16) | (hi_bits & jnp.uint32(0xFFFF0000))


def _unpack_bf16_pair(w):
    lo = lax.bitcast_convert_type(w << 16, F32).astype(BF16)
    hi = lax.bitcast_convert_type(w & jnp.uint32(0xFFFF0000), F32).astype(BF16)
    return lo, hi


def _outproj_kernel(a_ref, b_ref, c_ref, d_ref, x_ref, w_ref, mg_ref, g_ref, bb_ref, rw_ref, rb_ref, *rest):
    h_ref, hb_ref, ti_ref, tg_ref = rest[-4:]
    y = None
    for i, m_ref in enumerate((a_ref, b_ref, c_ref, d_ref)):
        m = m_ref[...]
        r = lax.rsqrt(jnp.mean(m * m, axis=-1, keepdims=True) + EPS)
        nrm = (m * r * mg_ref[:, i * GROUP_W:(i + 1) * GROUP_W]).astype(BF16)
        part = jnp.dot(nrm, w_ref[i * GROUP_W:(i + 1) * GROUP_W, :], preferred_element_type=F32)
        y = part if y is None else y + part
    h = _layer_norm(DEEPNORM_ALPHA * x_ref[...] + y, g_ref[...], bb_ref[...])
    h_ref[...] = h
    packed = _pack_bf16_pair(h[:, :D_MODEL // 2], h[:, D_MODEL // 2:])
    for i in range(h.shape[0] // SUBLANES):
        rows = slice(i * SUBLANES, (i + 1) * SUBLANES)
        hb_ref[rows] = _rows_to_tiles([packed[rows, s * LANES:(s + 1) * LANES] for s in range(ROW_TILES_X)])
    lg = jnp.dot(h.astype(BF16), rw_ref[...].astype(BF16), preferred_element_type=F32) + rb_ref[...]
    lane = lax.broadcasted_iota(jnp.int32, lg.shape, 1)
    ti = jnp.zeros(lg.shape, jnp.int32)
    tv = jnp.full(lg.shape, NEG, F32)
    for k in range(TOP_K):
        mx = jnp.max(lg, axis=-1, keepdims=True)
        idx = jnp.min(jnp.where(lg == mx, lane, LANES), axis=-1, keepdims=True)
        ti = jnp.where(lane == k, idx, ti)
        tv = jnp.where(lane == k, mx, tv)
        lg = jnp.where(lane == idx, NEG, lg)
    e = jnp.exp(tv - jnp.max(tv, axis=-1, keepdims=True))
    tg_ref[...] = e / jnp.sum(e, axis=-1, keepdims=True)
    ti_ref[...] = ti


def _outproj(a, b, c, d, x_all, w, mg, g, bb, rw, rb, tm, row0, prev=None):
    t_all = x_all.shape[0]
    n = a.shape[0]
    off = row0 // tm
    row = lambda w_: pl.BlockSpec((tm, w_), lambda i: (i, 0))
    row_off = lambda w_: pl.BlockSpec((tm, w_), lambda i: (i + off, 0))
    whole = lambda arr: pl.BlockSpec(arr.shape, lambda i: (0,) * arr.ndim)
    prev = () if prev is None else tuple(prev)
    n_in = 11
    return pl.pallas_call(
        _outproj_kernel,
        grid=(n // tm,),
        in_specs=[row(GROUP_W)] * 4 + [row_off(D_MODEL)] + [whole(arr) for arr in (w, mg, g, bb, rw, rb)]
                 + [pl.BlockSpec(memory_space=pl.ANY)] * len(prev),
        out_specs=[row_off(D_MODEL), pl.BlockSpec((tm, ROW_TILES_X, LANES), lambda i: (i + off, 0, 0)),
                   row_off(LANES), row_off(LANES)],
        out_shape=[jax.ShapeDtypeStruct((t_all, D_MODEL), F32),
                   jax.ShapeDtypeStruct((t_all, ROW_TILES_X, LANES), jnp.uint32),
                   jax.ShapeDtypeStruct((t_all, LANES), jnp.int32), jax.ShapeDtypeStruct((t_all, LANES), F32)],
        input_output_aliases={n_in + i: i for i in range(len(prev))},
        compiler_params=_cp(("arbitrary",)),
        name="outproj_ln_router",
    )(a, b, c, d, x_all, w, mg, g, bb, rw, rb, *prev)


def _moe_kernel(ge_ref, gb_ref, gr_ref, x_ref, wg_ref, wu_ref, wd_ref, bg_ref, bu_ref, bd_ref, y_ref,
                xb, yacc, wgu, wdb):
    del ge_ref, gb_ref
    g = pl.program_id(0)
    f = pl.program_id(1)
    nf = pl.num_programs(1)
    rows = gr_ref[g]
    half = D_MODEL // 2

    @pl.when(rows > 0)
    def _():
        @pl.when(f == 0)
        def _():
            def unpack16(i, carry):
                r0 = pl.multiple_of(i * 2 * SUBLANES, 2 * SUBLANES)
                top = _tiles_to_rows(x_ref[pl.ds(r0, SUBLANES)])
                bot = _tiles_to_rows(x_ref[pl.ds(r0 + SUBLANES, SUBLANES)])
                for s in range(ROW_TILES_X):
                    lo, hi = _unpack_bf16_pair(jnp.concatenate([top[s], bot[s]], axis=0))
                    xb[pl.ds(r0, 2 * SUBLANES), s * LANES:(s + 1) * LANES] = lo
                    xb[pl.ds(r0, 2 * SUBLANES), half + s * LANES:half + (s + 1) * LANES] = hi
                return carry

            lax.fori_loop(0, MOE_R // (2 * SUBLANES), unpack16, 0, unroll=2)
            yacc[...] = jnp.broadcast_to(bd_ref[...], yacc.shape)

        wgu[:, :MOE_TF] = wg_ref[...].astype(BF16)
        wgu[:, MOE_TF:] = wu_ref[...].astype(BF16)
        wdb[...] = wd_ref[...].astype(BF16)

        def ffn(nrows):
            hgu = jnp.dot(xb[0:nrows, :], wgu[...], preferred_element_type=F32)
            gate = jnp.minimum(hgu[:, :MOE_TF] + bg_ref[...], SWIGLU_LIMIT)
            up = jnp.clip(hgu[:, MOE_TF:] + bu_ref[...], -SWIGLU_LIMIT, SWIGLU_LIMIT)
            act = (up + 1.0) * (gate * _sigmoid(SWIGLU_ALPHA * gate))
            yacc[0:nrows, :] += jnp.dot(act.astype(BF16), wdb[...], preferred_element_type=F32)

        @pl.when(rows > MOE_SMALL)
        def _():
            ffn(MOE_R)

        @pl.when(rows <= MOE_SMALL)
        def _():
            ffn(MOE_SMALL)

        @pl.when(f == nf - 1)
        def _():
            def to_tiles(i, carry):
                r0 = pl.multiple_of(i * SUBLANES, SUBLANES)
                for t0 in range(0, ROW_TILES_Y, SUBLANES):
                    y_ref[pl.ds(r0, SUBLANES), t0:t0 + SUBLANES, :] = _rows_to_tiles(
                        [yacc[pl.ds(r0, SUBLANES), (t0 + s) * LANES:(t0 + s + 1) * LANES] for s in range(SUBLANES)])
                return carry

            lax.fori_loop(0, MOE_R // SUBLANES, to_tiles, 0, unroll=2)


def _moe_ffn(x_rows, w_gu, b_gu, w_dn, b_dn, layer, ge, gb, gr):
    ng = ge.shape[0]
    d = D_MODEL
    nf = D_FF // MOE_TF
    r = MOE_R
    depth = w_gu.shape[0]

    def fidx(g, f, gr_ref):
        return jnp.where(gr_ref[g] > 0, f, nf - 1)

    in_specs = [
        pl.BlockSpec((r, ROW_TILES_X, LANES), lambda g, f, ge_, gb_, gr_: (gb_[g], 0, 0)),
        pl.BlockSpec((None, None, d, MOE_TF), lambda g, f, ge_, gb_, gr_: (layer, ge_[g], 0, fidx(g, f, gr_))),
        pl.BlockSpec((None, None, d, MOE_TF), lambda g, f, ge_, gb_, gr_: (layer, ge_[g], 0, nf + fidx(g, f, gr_))),
        pl.BlockSpec((None, None, MOE_TF, d), lambda g, f, ge_, gb_, gr_: (layer, ge_[g], fidx(g, f, gr_), 0)),
        pl.BlockSpec((None, None, 1, MOE_TF), lambda g, f, ge_, gb_, gr_: (layer, ge_[g], 0, fidx(g, f, gr_))),
        pl.BlockSpec((None, None, 1, MOE_TF), lambda g, f, ge_, gb_, gr_: (layer, ge_[g], 0, nf + fidx(g, f, gr_))),
        pl.BlockSpec((None, None, 1, d), lambda g, f, ge_, gb_, gr_: (layer, ge_[g], 0, 0)),
    ]
    b_gu4 = b_gu.reshape(depth, N_EXPERTS, 1, -1)
    return pl.pallas_call(
        _moe_kernel,
        grid_spec=pltpu.PrefetchScalarGridSpec(
            num_scalar_prefetch=3,
            grid=(ng, nf),
            in_specs=in_specs,
            out_specs=pl.BlockSpec((r, ROW_TILES_Y, LANES), lambda g, f, ge_, gb_, gr_: (gb_[g], 0, 0),
                                   pipeline_mode=pl.Buffered(1)),
            scratch_shapes=[pltpu.VMEM((r, d), BF16), pltpu.VMEM((r, d), F32), pltpu.VMEM((d, 2 * MOE_TF), BF16),
                            pltpu.VMEM((MOE_TF, d), BF16)]),
        out_shape=jax.ShapeDtypeStruct((ng * r, ROW_TILES_Y, LANES), F32),
        compiler_params=_cp(("arbitrary", "arbitrary")),
        name="moe_ffn",
    )(ge, gb, gr, x_rows, w_gu, w_gu, w_dn, b_gu4, b_gu4, b_dn.reshape(depth, N_EXPERTS, 1, -1))


def _moe_plan(top_i, n_tok):
    n_pairs = n_tok * TOP_K
    r = MOE_R
    ng = n_pairs // r + N_EXPERTS
    flat_e = top_i.reshape(-1)
    onehot = (flat_e[:, None] == jnp.arange(N_EXPERTS, dtype=jnp.int32)[None, :]).astype(jnp.int32)
    csum = jnp.cumsum(onehot, axis=0)
    counts = csum[-1]
    rank = jnp.sum(onehot * csum, axis=1) - 1
    ngrp = (counts + r - 1) // r
    grp_end = jnp.cumsum(ngrp)
    grp_start = grp_end - ngrp
    dest = jnp.sum(onehot * grp_start[None, :], axis=1) * r + rank
    pair_tok = jnp.arange(n_pairs, dtype=jnp.int32) // TOP_K
    row_tok = jnp.zeros((ng * r,), jnp.int32).at[dest].set(pair_tok, unique_indices=True)
    total = grp_end[-1]
    gidx = jnp.arange(ng, dtype=jnp.int32)
    valid = gidx < total
    gclamp = jnp.minimum(gidx, total - 1)
    ge = jnp.sum(grp_end[None, :] <= gclamp[:, None], axis=1).astype(jnp.int32)
    local = gclamp - grp_start[ge]
    gr = jnp.where(valid, jnp.clip(counts[ge] - local * r, 0, r), 0).astype(jnp.int32)
    return row_tok, dest.reshape(n_tok, TOP_K).astype(jnp.int32), ge, gclamp.astype(jnp.int32), gr, total


GATHER_UNROLL = 8


def _gather_kernel(cnt_ref, rows_ref, idx_ref, src, out_ref, idx_s, sem_i, sem_g):
    g = pl.program_id(0)
    n = cnt_ref[0]
    u = GATHER_UNROLL

    @pl.when(g < n)
    def _():
        rows = rows_ref[g]
        cp = pltpu.make_async_copy(idx_ref, idx_s, sem_i)
        cp.start()

        @pl.when(rows < out_ref.shape[0])
        def _():
            out_ref[...] = jnp.zeros(out_ref.shape, out_ref.dtype)

        cp.wait()

        def fetch(i, j):
            pltpu.make_async_copy(src.at[pl.ds(idx_s[0, i, j], 1)], out_ref.at[pl.ds(i * u + j, 1)], sem_g).start()

        nfull = rows // u

        def full(i, carry):
            for j in range(u):
                fetch(i, j)
            return carry

        lax.fori_loop(0, nfull, full, 0)

        def tail(j, carry):
            fetch(nfull, j)
            return carry

        lax.fori_loop(0, rows - nfull * u, tail, 0)

        @pl.when(rows > 0)
        def _():
            pltpu.make_async_copy(out_ref.at[pl.ds(0, rows)], out_ref.at[pl.ds(0, rows)], sem_g).wait()


def _gather_rows(src, idx, cnt, rows, ch):
    p = idx.shape[0]
    steps = p // ch
    u = GATHER_UNROLL
    nb = ch // u
    return pl.pallas_call(
        _gather_kernel,
        grid_spec=pltpu.PrefetchScalarGridSpec(
            num_scalar_prefetch=2,
            grid=(steps,),
            in_specs=[pl.BlockSpec((1, nb, u), lambda g, c, r: (g, 0, 0)), pl.BlockSpec(memory_space=pl.ANY)],
            out_specs=pl.BlockSpec((ch,) + src.shape[1:], lambda g, c, r: (jnp.minimum(g, c[0] - 1), 0, 0)),
            scratch_shapes=[pltpu.SMEM((1, nb, u), jnp.int32), pltpu.SemaphoreType.DMA, pltpu.SemaphoreType.DMA]),
        out_shape=jax.ShapeDtypeStruct((p,) + src.shape[1:], src.dtype),
        compiler_params=_cp(("arbitrary",)),
        name="row_gather",
    )(cnt, rows, idx.reshape(steps, nb, u), src)


def _combine_kernel(y0_ref, y1_ref, y2_ref, y3_ref, tg_ref, h_ref, g_ref, b_ref, o_ref, ffn):
    y_refs = (y0_ref, y1_ref, y2_ref, y3_ref)

    def rows8(i, carry):
        r0 = pl.multiple_of(i * SUBLANES, SUBLANES)
        tg = tg_ref[pl.ds(r0, SUBLANES), :]
        for t0 in range(0, ROW_TILES_Y, SUBLANES):
            acc = None
            for k, y_ref in enumerate(y_refs):
                pieces = _tiles_to_rows(y_ref[pl.ds(r0, SUBLANES), t0:t0 + SUBLANES, :])
                gk = tg[:, k:k + 1]
                acc = [gk * p for p in pieces] if acc is None else [a + gk * p for a, p in zip(acc, pieces)]
            for s in range(SUBLANES):
                ffn[pl.ds(r0, SUBLANES), (t0 + s) * LANES:(t0 + s + 1) * LANES] = acc[s]
        return carry

    lax.fori_loop(0, h_ref.shape[0] // SUBLANES, rows8, 0, unroll=2)
    o_ref[...] = _layer_norm(DEEPNORM_ALPHA * h_ref[...] + ffn[...], g_ref[...], b_ref[...])


def _combine(yg, tg, h, g, b, tm):
    t, d = h.shape
    ysp = lambda k: pl.BlockSpec((None, tm, ROW_TILES_Y, LANES), lambda i: (k, i, 0, 0))
    return pl.pallas_call(
        _combine_kernel,
        grid=(pl.cdiv(t, tm),),
        in_specs=[ysp(0), ysp(1), ysp(2), ysp(3),
                  pl.BlockSpec((tm, LANES), lambda i: (i, 0)),
                  pl.BlockSpec((tm, d), lambda i: (i, 0)),
                  pl.BlockSpec((1, d), lambda i: (0, 0)),
                  pl.BlockSpec((1, d), lambda i: (0, 0))],
        out_specs=pl.BlockSpec((tm, d), lambda i: (i, 0)),
        out_shape=jax.ShapeDtypeStruct((t, d), F32),
        scratch_shapes=[pltpu.VMEM((tm, d), F32)],
        compiler_params=_cp(("arbitrary",)),
        name="combine_ln2",
    )(yg, yg, yg, yg, tg, h, g, b)


def _lane_row(vals, offset, width=LANES):
    row = jnp.zeros((width,), F32)
    return lax.dynamic_update_slice(row, vals.astype(F32), (offset,)).reshape(1, width)


def _layer(x_all, P, past, nb, seq, nd, kv_prev):
    tp = nb * seq
    n_tok = tp + nd
    rb = tp // nd
    w_in = _prep_w_in(P['w_in'])
    hp_ = _matmul(x_all, w_in, 512, IN_TN)
    hs_ = hp_[tp:]

    hp = jnp.concatenate([_lane_row(P['ssd_dt_bias'], MISC_DT), _lane_row(P['ssd_a_log'], MISC_DT),
                          _lane_row(P['ssd_d'], MISC_DT), _lane_row(P['b_forget'], MISC_F),
                          jnp.zeros((4, LANES), F32)], axis=0)
    bf_row = hp[3:4]
    r1 = lambda a: a.reshape(1, -1).astype(F32)

    logf, c = _fox_prep(hp_, bf_row, nb, seq)
    a_out = _fox_prompt(hp_, c, nb, seq)
    b_out, ssm_p, sconv_p = _ssd_prompt(hp_, P['ssd_conv_w'], r1(P['ssd_conv_b']), hp, nb, seq)
    bst = jnp.pad(P['gmlp_bs'].T, ((0, 0), (0, LANES - GMLP_GROUPS)))
    c_out = _gmlp_prompt(hp_, r1(P['gmlp_ln_g']), r1(P['gmlp_ln_b']), P['gmlp_ws'], bst, tp)
    ccw = jnp.pad(P['conf_conv_w'], ((0, CONF_HALO - CONF_KERNEL), (0, 0)))
    d_out, conf_p = _conf_prompt(hp_, ccw, r1(P['conf_conv_b']), r1(P['conf_ln_g']), r1(P['conf_ln_b']), nb, seq)

    kv = _kv_state(hp_, tp, past['layer'], past['depth'], kv_prev)
    logf_p = logf[:, :FOX_HEADS].reshape(nb, seq, FOX_HEADS)
    state_p = (logf_p, ssm_p, sconv_p[:, SUBLANES - (SSD_CONV - 1):, :],
               conf_p[:, CONF_HALO - (CONF_KERNEL - 1):, :])

    w00 = jnp.repeat(P['gmlp_ws'][:, 0, 0], GROUP_W // GMLP_GROUPS).reshape(1, GROUP_W)
    b0 = jnp.repeat(P['gmlp_bs'][:, 0], GROUP_W // GMLP_GROUPS).reshape(1, GROUP_W)
    scw = jnp.pad(P['ssd_conv_w'], ((0, SUBLANES - SSD_CONV), (0, 0)))
    c_s, vrow_s, d_s, glu_s, xc_s, sm_s = _dec_rows(
        hp_, rb, nd, r1(P['gmlp_ln_g']), r1(P['gmlp_ln_b']), w00, b0,
        ccw, r1(P['conf_conv_b']), r1(P['conf_ln_g']), r1(P['conf_ln_b']), past['state_conv'],
        scw, r1(P['ssd_conv_b']), past['state_ssd_conv'], hp)
    yt, ssm_s = _dec_ssm(xc_s[:, :GROUP_W].T, hs_[:, COL_Z:COL_Z + GROUP_W].T, xc_s[:, GROUP_W:], sm_s, hp,
                         past['state_ssm'])
    b_s = yt.T
    a_s = _fox_decode(hp_, rb, sm_s, past['cache_k'], past['cache_v'], past['cache_lft'], past['page_table'],
                      past['layer'])
    k_s = hs_[:, COL_K:COL_K + GROUP_W].reshape(nd, 1, FOX_HEADS, FOX_HEAD_DIM)
    v_s = hs_[:, COL_V:COL_V + GROUP_W].reshape(nd, 1, FOX_HEADS, FOX_HEAD_DIM)
    logf_s = sm_s[:, :FOX_HEADS].reshape(nd, 1, FOX_HEADS)
    sconv_s = jnp.concatenate([past['state_ssd_conv'][:, 1:], hs_[:, None, COL_XBC:COL_XBC + SSD_CONV_DIM]], axis=1)
    conf_s = jnp.concatenate([past['state_conv'][:, 1:], glu_s[:, None, :]], axis=1)
    state_s = (k_s, v_s, logf_s, ssm_s, sconv_s, conf_s, vrow_s.reshape(nd, 1, GROUP_W))

    w_out = P['w_out'].astype(BF16)
    rw = jnp.pad(P['router_w'], ((0, 0), (0, LANES - N_EXPERTS)))
    rb = jnp.concatenate([P['router_b'].astype(F32), jnp.full((LANES - N_EXPERTS,), NEG, F32)]).reshape(1, LANES)
    op_args = (w_out, r1(P['mix_norm_g']), r1(P['ln1_g']), r1(P['ln1_b']), rw, rb)
    outs = _outproj(a_out, b_out, c_out, d_out, x_all, *op_args, tm=256, row0=0)
    h_all, hpk_all, ti_all, tg_all = _outproj(a_s, b_s, c_s, d_s, x_all, *op_args, tm=nd, row0=tp, prev=outs)

    row_tok, dest, ge, gb, gr, total = _moe_plan(ti_all[:, :TOP_K], n_tok)
    x_rows = _gather_rows(hpk_all, row_tok, total.reshape(1), gr, MOE_R)
    y_rows = _moe_ffn(x_rows, past['exp_w_gu'], past['exp_b_gu'], past['exp_w_dn'], past['exp_b_dn'], past['layer'],
                      ge, gb, gr)
    n_pairs = n_tok * TOP_K
    steps = max(s for s in range(1, 33) if (n_pairs // SUBLANES) % s == 0)
    yg = _gather_rows(y_rows, dest.T.reshape(-1), jnp.full((1,), steps, jnp.int32),
                      jnp.full((steps,), n_pairs // steps, jnp.int32), n_pairs // steps)
    o_all = _combine(yg.reshape(TOP_K, n_tok, ROW_TILES_Y, LANES), tg_all, h_all, r1(P['ln2_g']), r1(P['ln2_b']),
                     tm=256)
    return o_all, state_p, state_s, kv


PARAM_NAMES = ('w_in', 'b_forget', 'ssd_conv_w', 'ssd_conv_b', 'ssd_dt_bias', 'ssd_a_log', 'ssd_d', 'gmlp_ln_g',
               'gmlp_ln_b', 'gmlp_ws', 'gmlp_bs', 'conf_conv_w', 'conf_conv_b', 'conf_ln_g', 'conf_ln_b',
               'mix_norm_g', 'w_out', 'ln1_g', 'ln1_b', 'router_w', 'router_b', 'exp_w_gu', 'exp_b_gu',
               'exp_w_dn', 'exp_b_dn', 'ln2_g', 'ln2_b')


def kernel(x_prompt, x_sample, cache_k, cache_v, cache_logf, state_ssm, state_ssd_conv, state_conv, page_table, w_in, b_forget, ssd_conv_w, ssd_conv_b, ssd_dt_bias, ssd_a_log, ssd_d, gmlp_ln_g, gmlp_ln_b, gmlp_ws, gmlp_bs, conf_conv_w, conf_conv_b, conf_ln_g, conf_ln_b, mix_norm_g, w_out, ln1_g, ln1_b, router_w, router_b, exp_w_gu, exp_b_gu, exp_w_dn, exp_b_dn, ln2_g, ln2_b):
    weights = (w_in, b_forget, ssd_conv_w, ssd_conv_b, ssd_dt_bias, ssd_a_log, ssd_d, gmlp_ln_g, gmlp_ln_b, gmlp_ws,
               gmlp_bs, conf_conv_w, conf_conv_b, conf_ln_g, conf_ln_b, mix_norm_g, w_out, ln1_g, ln1_b, router_w,
               router_b, exp_w_gu, exp_b_gu, exp_w_dn, exp_b_dn, ln2_g, ln2_b)
    nb, seq, d = x_prompt.shape
    nd = x_sample.shape[0]
    depth = w_in.shape[0]
    n_pool = cache_k.shape[1]
    ck = cache_k.reshape(depth, n_pool, PAGE_SIZE, GROUP_W)
    cv = cache_v.reshape(depth, n_pool, PAGE_SIZE, GROUP_W)
    clft = cache_logf
    tp = nb * seq
    x_all = jnp.concatenate([x_prompt.reshape(tp, d), x_sample.reshape(nd, d)], axis=0)
    st_p, st_s = [], []
    kv = None
    big = ('exp_w_gu', 'exp_b_gu', 'exp_w_dn', 'exp_b_dn')
    for l in range(depth):
        P = {n: w[l] for n, w in zip(PARAM_NAMES, weights) if n not in big}
        past = dict(cache_k=ck, cache_v=cv, cache_lft=clft, layer=l, depth=depth, state_ssm=state_ssm[l],
                    state_ssd_conv=state_ssd_conv[l], state_conv=state_conv[l], page_table=page_table,
                    exp_w_gu=exp_w_gu, exp_b_gu=exp_b_gu, exp_w_dn=exp_w_dn, exp_b_dn=exp_b_dn)
        x_all, sp, ss, kv = _layer(x_all, P, past, nb, seq, nd, kv)
        st_p.append(sp)
        st_s.append(ss)
    stack = lambda sts, i: jnp.stack([s[i] for s in sts], axis=0)
    kv_shape = (depth, nb, seq, FOX_HEADS, FOX_HEAD_DIM)
    return (x_all[:tp].reshape(nb, seq, d), x_all[tp:].reshape(nd, 1, d),
            kv[0].reshape(kv_shape), kv[1].reshape(kv_shape), stack(st_p, 0),
            stack(st_s, 0), stack(st_s, 1), stack(st_s, 2),
            stack(st_p, 1), stack(st_s, 3), stack(st_p, 2), stack(st_s, 4),
            stack(st_p, 3), stack(st_s, 5), stack(st_s, 6))
```
